```python
import math
import jax, jax.numpy as jnp
from jax import lax
import numpy as np

D_MODEL = 1024
BATCH = 16
SEQ = 2048
DEPTH = 2
DEC_BATCH = 32
DEC_SEQ = 8
PAST_LEN = 16384
PAGE_SIZE = 128

N_HEADS_ATT = 8
HEAD_DIM = 64
W_ATT = N_HEADS_ATT * HEAD_DIM
W_CONV = D_MODEL // 4
CONV_WIDTH = 3
W_SSM = D_MODEL - W_ATT - W_CONV
SSM_GROUP = 16
N_SSM_GROUPS = W_SSM // SSM_GROUP
SSM_STATE = 64
MOBA_BLOCK = 256
MOBA_TOP_K = 3
QUERY_CHUNK = 16
ATT_SCALE = HEAD_DIM ** -0.5
D_FF = 2816
N_EXPERTS = 8
EXPERT_TOP_K = 2
D_FF_EXPERT = D_FF // EXPERT_TOP_K
IN_COLS = 3 * W_ATT + 3 * W_CONV + W_SSM
SPLIT_POINTS = (W_ATT, 2 * W_ATT, 3 * W_ATT, 3 * W_ATT + W_CONV, 3 * W_ATT + 2 * W_CONV, 3 * W_ATT + 3 * W_CONV)
RMS_EPS = 1e-6

kernel_name = 'hymba_moba_conv_s5_moe_decode_step'


def rmsnorm(x, g):
    xf = x.astype(jnp.float32)
    y = xf * lax.rsqrt(jnp.mean(xf * xf, axis=-1, keepdims=True) + RMS_EPS)
    return (y * g.astype(jnp.float32)).astype(x.dtype)


def alibi_slopes():
    return jnp.asarray(2.0 ** (-8.0 * np.arange(1, N_HEADS_ATT + 1) / N_HEADS_ATT), jnp.float32)


def alibi_logits(raw, qpos, kpos, mask):
    dist = (qpos - kpos).astype(jnp.float32)
    logits = raw * ATT_SCALE - alibi_slopes()[:, None] * dist
    if mask is None:
        return logits
    return jnp.where(mask, logits, -jnp.inf)


def moba_prompt(q, k, v):
    b, s, h, dh = q.shape
    nb = -(-s // MOBA_BLOCK)
    sp = nb * MOBA_BLOCK
    padw = ((0, 0), (0, sp - s), (0, 0), (0, 0))
    qp, kp, vp = jnp.pad(q, padw), jnp.pad(k, padw), jnp.pad(v, padw)
    kb = kp.reshape(b, nb, MOBA_BLOCK, h, dh)
    vb = vp.reshape(b, nb, MOBA_BLOCK, h, dh)
    k_mean = jnp.mean(kb.astype(jnp.float32), axis=2)
    pos = jnp.arange(sp, dtype=jnp.int32)
    qblk = pos // MOBA_BLOCK
    gate = jnp.einsum('bthd,bnhd->bthn', qp.astype(jnp.float32), k_mean)
    past_ok = jnp.arange(nb, dtype=jnp.int32)[None, :] < qblk[:, None]
    gate = jnp.where(past_ok[None, :, None, :], gate, -jnp.inf)
    _, sel = lax.top_k(gate, min(MOBA_TOP_K, nb))
    sel_ok = sel < qblk[None, :, None, None]
    own = jnp.broadcast_to(qblk[None, :, None, None], (b, sp, h, 1))
    idx = jnp.concatenate([sel, own], axis=-1)
    ok = jnp.concatenate([sel_ok, jnp.ones((b, sp, h, 1), bool)], axis=-1)
    n = idx.shape[-1]
    nc = sp // QUERY_CHUNK

    def to_chunks(a):
        return jnp.moveaxis(a.reshape((b, nc, QUERY_CHUNK) + a.shape[2:]), 1, 0)

    bi = jnp.arange(b)[:, None, None, None]
    hi = jnp.arange(h)[None, None, :, None]
    offs = jnp.arange(MOBA_BLOCK, dtype=jnp.int32)

    def chunk(args):
        qc, ic, oc, pc = args
        kg = kb[bi, ic, :, hi, :].reshape(b, QUERY_CHUNK, h, n * MOBA_BLOCK, dh)
        vg = vb[bi, ic, :, hi, :].reshape(b, QUERY_CHUNK, h, n * MOBA_BLOCK, dh)
        kpos = (ic[..., None] * MOBA_BLOCK + offs).reshape(b, QUERY_CHUNK, h, n * MOBA_BLOCK)
        qpos = pc[None, :, None, None]
        mask = jnp.repeat(oc, MOBA_BLOCK, axis=-1) & (kpos <= qpos)
        raw = jnp.einsum('bthd,bthmd->bthm', qc, kg, preferred_element_type=jnp.float32)
        p = jax.nn.softmax(alibi_logits(raw, qpos, kpos, mask), axis=-1)
        return jnp.einsum('bthm,bthmd->bthd', p.astype(vg.dtype), vg)

    out = lax.map(chunk, (to_chunks(qp), to_chunks(idx), to_chunks(ok), pos.reshape(nc, QUERY_CHUNK)))
    out = jnp.moveaxis(out, 0, 1).reshape(b, sp, h, dh)[:, :s]
    return out.astype(q.dtype)


def moba_sample(q, k, v, cache_k, cache_v, l, page_table):
    b, t, h, dh = q.shape
    n_pages = page_table.shape[1]
    ps = cache_k.shape[2]
    past = n_pages * ps
    ppb = MOBA_BLOCK // ps
    nbp = past // MOBA_BLOCK
    r = past - nbp * MOBA_BLOCK
    qpos = (past + jnp.arange(t, dtype=jnp.int32))[None, :, None, None]
    logit_parts = []
    if nbp > 0:
        k_past = cache_k[l, page_table[:, :nbp * ppb]]
        k_mean = jnp.mean(k_past.reshape(b, nbp, MOBA_BLOCK, h, dh).astype(jnp.float32), axis=2)
        gate = jnp.einsum('bthd,bnhd->bthn', q.astype(jnp.float32), k_mean)
        n_sel = min(MOBA_TOP_K, nbp)
        _, sel = lax.top_k(gate, n_sel)
        bi = jnp.arange(b)[:, None, None, None, None]
        hi = jnp.arange(h)[None, None, :, None, None]
        phys = page_table[bi, sel[..., None] * ppb + jnp.arange(ppb, dtype=jnp.int32)]
        k_sel = cache_k[l, phys, :, hi, :].reshape(b, t, h, n_sel * MOBA_BLOCK, dh)
        v_sel = cache_v[l, phys, :, hi, :].reshape(b, t, h, n_sel * MOBA_BLOCK, dh)
        kpos_sel = (sel[..., None] * MOBA_BLOCK + jnp.arange(MOBA_BLOCK, dtype=jnp.int32)).reshape(b, t, h, n_sel * MOBA_BLOCK)
        raw = jnp.einsum('bthd,bthmd->bthm', q, k_sel, preferred_element_type=jnp.float32)
        logit_parts.append(alibi_logits(raw, qpos, kpos_sel, None))
    if r > 0:
        own_pages = page_table[:, nbp * ppb: nbp * ppb + r // ps]
        k_own = jnp.concatenate([cache_k[l, own_pages].reshape(b, r, h, dh), k.astype(cache_k.dtype)], axis=1)
        v_own = jnp.concatenate([cache_v[l, own_pages].reshape(b, r, h, dh), v.astype(cache_v.dtype)], axis=1)
    else:
        k_own, v_own = k, v
    kpos_own = (nbp * MOBA_BLOCK + jnp.arange(r + t, dtype=jnp.int32))[None, None, None, :]
    raw_own = jnp.einsum('bthd,bshd->bths', q, k_own, preferred_element_type=jnp.float32)
    logit_parts.append(alibi_logits(raw_own, qpos, kpos_own, kpos_own <= qpos))
    p = jax.nn.softmax(jnp.concatenate(logit_parts, axis=-1), axis=-1)
    m_total = p.shape[-1]
    out = jnp.einsum('bths,bshd->bthd', p[..., m_total - (r + t):].astype(v_own.dtype), v_own)
    if nbp > 0:
        out = out + jnp.einsum('bthm,bthmd->bthd', p[..., :n_sel * MOBA_BLOCK].astype(v_sel.dtype), v_sel)
    return out.astype(q.dtype)


def short_conv(bg, cg, xc, prev, w, bias):
    z = cg * xc
    zp = jnp.concatenate([prev.astype(z.dtype), z], axis=1)
    s = z.shape[1]
    y = bias
    for i in range(CONV_WIDTH):
        y = y + w[i] * zp[:, i:i + s]
    return bg * y, zp[:, zp.shape[1] - (CONV_WIDTH - 1):]


def _complex_affine_combine(e1, e2):
    a1r, a1i, b1r, b1i = e1
    a2r, a2i, b2r, b2i = e2
    return (a2r * a1r - a2i * a1i,
            a2r * a1i + a2i * a1r,
            a2r * b1r - a2i * b1i + b2r,
            a2r * b1i + a2i * b1r + b2i)


def s5_mixer(u, h0_re, h0_im, a_re, a_im, log_dt, b_re, b_im, c_re, c_im, d, w_glu):
    f32 = jnp.float32
    bsz, s, _ = u.shape
    uf = u.astype(f32).reshape(bsz, s, N_SSM_GROUPS, SSM_GROUP)
    ar, ai = a_re.astype(f32), a_im.astype(f32)
    dt = jnp.exp(log_dt.astype(f32))[:, None]
    mag = jnp.exp(dt * ar)
    abr, abi = mag * jnp.cos(dt * ai), mag * jnp.sin(dt * ai)
    den = ar * ar + ai * ai
    zr = ((abr - 1.0) * ar + abi * ai) / den
    zi = (abi * ar - (abr - 1.0) * ai) / den
    br, bim = b_re.astype(f32), b_im.astype(f32)
    bbr = zr[..., None] * br - zi[..., None] * bim
    bbi = zr[..., None] * bim + zi[..., None] * br
    xr = jnp.einsum('gnc,bsgc->bsgn', bbr, uf)
    xi = jnp.einsum('gnc,bsgc->bsgn', bbi, uf)
    h0r, h0i = h0_re.astype(f32), h0_im.astype(f32)
    xr = xr.at[:, 0].add(abr * h0r - abi * h0i)
    xi = xi.at[:, 0].add(abr * h0i + abi * h0r)
    elems = (jnp.broadcast_to(abr, xr.shape), jnp.broadcast_to(abi, xi.shape), xr, xi)
    _, _, hr, hi = lax.associative_scan(_complex_affine_combine, elems, axis=1)
    y = (jnp.einsum('gcn,bsgn->bsgc', c_re.astype(f32), hr)
         - jnp.einsum('gcn,bsgn->bsgc', c_im.astype(f32), hi)
         + d.astype(f32).reshape(N_SSM_GROUPS, SSM_GROUP) * uf)
    y = jax.nn.gelu(y.reshape(bsz, s, W_SSM))
    y = y * jax.nn.sigmoid(y @ w_glu.astype(f32))
    return y.astype(u.dtype), hr[:, -1], hi[:, -1]


def swiglu(h, wg, wu, wd):
    return (jax.nn.silu(h @ wg) * (h @ wu)) @ wd


def moe_ffn(h, w_router, b_router, wg, wu, wd):
    logits = (h @ w_router).astype(jnp.float32) + b_router.astype(jnp.float32)
    top_val, top_idx = lax.top_k(logits, EXPERT_TOP_K)
    gates = jax.nn.softmax(top_val, axis=-1)
    comb = jnp.sum(jax.nn.one_hot(top_idx, N_EXPERTS, dtype=jnp.float32) * gates[..., None], axis=-2)
    out = jnp.zeros_like(h)
    for e in range(N_EXPERTS):
        out = out + comb[..., e:e + 1].astype(h.dtype) * swiglu(h, wg[e], wu[e], wd[e])
    return out


def run_trunk(x, attend, conv0, ssm0_re, ssm0_im, w):
    ks, vs, convs, hres, hims = [], [], [], [], []
    bsz, s = x.shape[0], x.shape[1]
    for l in range(DEPTH):
        h = rmsnorm(x, w['g_mix_norm'][l])
        p = h @ w['w_in'][l]
        q, k, v, cb, cc, cx, u = jnp.split(p, SPLIT_POINTS, axis=-1)
        q = q.reshape(bsz, s, N_HEADS_ATT, HEAD_DIM)
        k = k.reshape(bsz, s, N_HEADS_ATT, HEAD_DIM)
        v = v.reshape(bsz, s, N_HEADS_ATT, HEAD_DIM)
        att = attend(l, q, k, v).reshape(bsz, s, W_ATT)
        conv_out, conv_new = short_conv(cb, cc, cx, conv0[l], w['conv_w'][l], w['conv_b'][l])
        ssm_out, hr, hi = s5_mixer(u, ssm0_re[l], ssm0_im[l], w['ssm_a_re'][l], w['ssm_a_im'][l],
                                   w['ssm_log_dt'][l], w['ssm_b_re'][l], w['ssm_b_im'][l],
                                   w['ssm_c_re'][l], w['ssm_c_im'][l], w['ssm_d'][l], w['ssm_w_glu'][l])
        g = w['g_branch'][l]
        mixed = jnp.concatenate([rmsnorm(att, g[:W_ATT]),
                                 rmsnorm(conv_out, g[W_ATT:W_ATT + W_CONV]),
                                 rmsnorm(ssm_out, g[W_ATT + W_CONV:])], axis=-1)
        x = x + mixed @ w['w_out'][l]
        h2 = rmsnorm(x, w['g_ffn_norm'][l])
        j = l // 2
        if l % 2 == 0:
            f = swiglu(h2, w['w_ffn_gate'][j], w['w_ffn_up'][j], w['w_ffn_down'][j])
        else:
            f = moe_ffn(h2, w['w_router'][j], w['b_router'][j], w['w_exp_gate'][j], w['w_exp_up'][j], w['w_exp_down'][j])
        x = x + f
        ks.append(k)
        vs.append(v)
        convs.append(conv_new)
        hres.append(hr)
        hims.append(hi)
    y = rmsnorm(x, w['g_final'])
    return y, jnp.stack(ks), jnp.stack(vs), jnp.stack(convs), jnp.stack(hres), jnp.stack(hims)


def setup_inputs(seed: int = 0) -> dict:
    key = jax.random.key(seed)
    kit = iter(jax.random.split(key, 40))
    f32 = jnp.float32
    n_pages = PAST_LEN // PAGE_SIZE
    n_used = DEC_BATCH * n_pages
    n_pool = n_used + max(1, n_used // 4)
    n_dense = (DEPTH + 1) // 2
    n_moe = DEPTH // 2

    def nrm(shape, scale=1.0):
        return jax.random.normal(next(kit), shape, f32) * scale

    def gain(shape):
        return 1.0 + 0.01 * jax.random.normal(next(kit), shape, f32)

    x_prompt = nrm((BATCH, SEQ, D_MODEL))
    x_sample = nrm((DEC_BATCH, DEC_SEQ, D_MODEL))
    cache_k = nrm((DEPTH, n_pool, PAGE_SIZE, N_HEADS_ATT, HEAD_DIM))
    cache_v = nrm((DEPTH, n_pool, PAGE_SIZE, N_HEADS_ATT, HEAD_DIM))
    state_conv = nrm((DEPTH, DEC_BATCH, CONV_WIDTH - 1, W_CONV))
    state_ssm_re = nrm((DEPTH, DEC_BATCH, N_SSM_GROUPS, SSM_STATE), 0.3)
    state_ssm_im = nrm((DEPTH, DEC_BATCH, N_SSM_GROUPS, SSM_STATE), 0.3)
    page_table = jax.random.permutation(next(kit), n_pool)[:n_used].reshape(DEC_BATCH, n_pages).astype(jnp.int32)
    g_mix_norm = gain((DEPTH, D_MODEL))
    w_in = nrm((DEPTH, D_MODEL, IN_COLS), D_MODEL ** -0.5)
    g_branch = gain((DEPTH, D_MODEL))
    w_out = nrm((DEPTH, D_MODEL, D_MODEL), D_MODEL ** -0.5)
    conv_w = nrm((DEPTH, CONV_WIDTH, W_CONV), CONV_WIDTH ** -0.5)
    conv_b = nrm((DEPTH, W_CONV), 0.01)
    ssm_a_re = -0.5 + nrm((DEPTH, N_SSM_GROUPS, SSM_STATE), 0.01)
    ssm_a_im = jnp.pi * jnp.arange(SSM_STATE, dtype=f32) + nrm((DEPTH, N_SSM_GROUPS, SSM_STATE), 0.01)
    ssm_log_dt = jax.random.uniform(next(kit), (DEPTH, N_SSM_GROUPS), f32, math.log(1e-3), math.log(1e-1))
    ssm_b_re = nrm((DEPTH, N_SSM_GROUPS, SSM_STATE, SSM_GROUP), (2 * SSM_GROUP) ** -0.5)
    ssm_b_im = nrm((DEPTH, N_SSM_GROUPS, SSM_STATE, SSM_GROUP), (2 * SSM_GROUP) ** -0.5)
    ssm_c_re = nrm((DEPTH, N_SSM_GROUPS, SSM_GROUP, SSM_STATE), SSM_STATE ** -0.5)
    ssm_c_im = nrm((DEPTH, N_SSM_GROUPS, SSM_GROUP, SSM_STATE), SSM_STATE ** -0.5)
    ssm_d = nrm((DEPTH, W_SSM), 0.5)
    ssm_w_glu = nrm((DEPTH, W_SSM, W_SSM), W_SSM ** -0.5)
    g_ffn_norm = gain((DEPTH, D_MODEL))
    w_ffn_gate = nrm((n_dense, D_MODEL, D_FF), D_MODEL ** -0.5)
    w_ffn_up = nrm((n_dense, D_MODEL, D_FF), D_MODEL ** -0.5)
    w_ffn_down = nrm((n_dense, D_FF, D_MODEL), D_FF ** -0.5)
    w_router = nrm((n_moe, D_MODEL, N_EXPERTS), D_MODEL ** -0.5)
    b_router = nrm((n_moe, N_EXPERTS), 0.01)
    w_exp_gate = nrm((n_moe, N_EXPERTS, D_MODEL, D_FF_EXPERT), D_MODEL ** -0.5)
    w_exp_up = nrm((n_moe, N_EXPERTS, D_MODEL, D_FF_EXPERT), D_MODEL ** -0.5)
    w_exp_down = nrm((n_moe, N_EXPERTS, D_FF_EXPERT, D_MODEL), D_FF_EXPERT ** -0.5)
    g_final = gain((D_MODEL,))
    return {'x_prompt': x_prompt, 'x_sample': x_sample, 'cache_k': cache_k, 'cache_v': cache_v,
            'state_conv': state_conv, 'state_ssm_re': state_ssm_re, 'state_ssm_im': state_ssm_im,
            'page_table': page_table, 'g_mix_norm': g_mix_norm, 'w_in': w_in, 'g_branch': g_branch,
            'w_out': w_out, 'conv_w': conv_w, 'conv_b': conv_b, 'ssm_a_re': ssm_a_re, 'ssm_a_im': ssm_a_im,
            'ssm_log_dt': ssm_log_dt, 'ssm_b_re': ssm_b_re, 'ssm_b_im': ssm_b_im, 'ssm_c_re': ssm_c_re,
            'ssm_c_im': ssm_c_im, 'ssm_d': ssm_d, 'ssm_w_glu': ssm_w_glu, 'g_ffn_norm': g_ffn_norm,
            'w_ffn_gate': w_ffn_gate, 'w_ffn_up': w_ffn_up, 'w_ffn_down': w_ffn_down,
            'w_router': w_router, 'b_router': b_router, 'w_exp_gate': w_exp_gate, 'w_exp_up': w_exp_up,
            'w_exp_down': w_exp_down, 'g_final': g_final}


def reference(x_prompt, x_sample, cache_k, cache_v, state_conv, state_ssm_re, state_ssm_im, page_table,
              g_mix_norm, w_in, g_branch, w_out, conv_w, conv_b, ssm_a_re, ssm_a_im, ssm_log_dt,
              ssm_b_re, ssm_b_im, ssm_c_re, ssm_c_im, ssm_d, ssm_w_glu, g_ffn_norm,
              w_ffn_gate, w_ffn_up, w_ffn_down, w_router, b_router, w_exp_gate, w_exp_up, w_exp_down, g_final):
    w = {'g_mix_norm': g_mix_norm, 'w_in': w_in, 'g_branch': g_branch, 'w_out': w_out,
         'conv_w': conv_w, 'conv_b': conv_b, 'ssm_a_re': ssm_a_re, 'ssm_a_im': ssm_a_im,
         'ssm_log_dt': ssm_log_dt, 'ssm_b_re': ssm_b_re, 'ssm_b_im': ssm_b_im, 'ssm_c_re': ssm_c_re,
         'ssm_c_im': ssm_c_im, 'ssm_d': ssm_d, 'ssm_w_glu': ssm_w_glu, 'g_ffn_norm': g_ffn_norm,
         'w_ffn_gate': w_ffn_gate, 'w_ffn_up': w_ffn_up, 'w_ffn_down': w_ffn_down,
         'w_router': w_router, 'b_router': b_router, 'w_exp_gate': w_exp_gate, 'w_exp_up': w_exp_up,
         'w_exp_down': w_exp_down, 'g_final': g_final}
    bp = x_prompt.shape[0]
    conv0 = jnp.zeros((DEPTH, bp, CONV_WIDTH - 1, W_CONV), x_prompt.dtype)
    ssm0 = jnp.zeros((DEPTH, bp, N_SSM_GROUPS, SSM_STATE), jnp.float32)
    y_prompt, k_p, v_p, conv_p, sre_p, sim_p = run_trunk(
        x_prompt, lambda l, q, k, v: moba_prompt(q, k, v), conv0, ssm0, ssm0, w)
    y_sample, k_s, v_s, conv_s, sre_s, sim_s = run_trunk(
        x_sample, lambda l, q, k, v: moba_sample(q, k, v, cache_k, cache_v, l, page_table),
        state_conv, state_ssm_re, state_ssm_im, w)
    return (y_prompt, y_sample, k_p, v_p, conv_p, sre_p, sim_p, k_s, v_s, conv_s, sre_s, sim_s)
```

```python
import functools

import numpy as np
import jax
import jax.numpy as jnp
from jax import lax
from jax.experimental import pallas as pl
from jax.experimental.pallas import tpu as pltpu

F32 = jnp.float32
BF16 = jnp.bfloat16
HIGHEST = lax.Precision.HIGHEST

D_MODEL = 1024
N_HEADS = 8
HEAD_DIM = 64
W_ATT = N_HEADS * HEAD_DIM
W_CONV = 256
W_SSM = 256
CONV_WIDTH = 3
SSM_GROUP = 16
N_GROUPS = 16
SSM_STATE = 64
N_STATE = N_GROUPS * SSM_STATE
MOBA_BLOCK = 256
MOBA_TOP_K = 3
ATT_SCALE = HEAD_DIM ** -0.5
N_EXPERTS = 8
EXPERT_TOP_K = 2
RMS_EPS = 1e-6
NEG = -1e30

LANES = 128
SUBLANES = 8
VMEM_LIMIT = 56 * 1024 * 1024
ROW_TILE = 512
NT_DIMS = (((1,), (1,)), ((), ()))


def _cparams(sem):
    return pltpu.CompilerParams(dimension_semantics=sem, vmem_limit_bytes=VMEM_LIMIT)


def _rms(x):
    return x * lax.rsqrt(jnp.mean(x * x, axis=-1, keepdims=True) + RMS_EPS)


def _full(shape):
    n = len(shape)
    return pl.BlockSpec(shape, lambda *_: (0,) * n)


def _lane_column(x, idx):
    lane = lax.broadcasted_iota(jnp.int32, x.shape, 1)
    return jnp.sum(jnp.where(lane == idx, x, 0.0), axis=1, keepdims=True)


def _top_k_mask(score, n_cand, k):
    lane = lax.broadcasted_iota(jnp.int32, score.shape, 1)
    rank = jnp.zeros(score.shape, jnp.int32)
    for m in range(n_cand):
        sm = score[:, m:m + 1]
        rank += ((sm > score) | ((sm == score) & (m < lane))).astype(jnp.int32)
    return rank < k


def _mix_in_kernel(x_ref, g_ref, w_ref, wkvt_ref, q_ref, k_ref, v_ref, cb_ref, cc_ref, cx_ref, u_ref, *, kv_transposed):
    h = (_rms(x_ref[...]) * g_ref[...]).astype(BF16)
    outs = (q_ref, k_ref, v_ref, cb_ref, cc_ref, cx_ref, u_ref)
    widths = (W_ATT,) * 3 + (W_CONV,) * 3 + (W_SSM,)
    col = 0
    for idx, (ref, width) in enumerate(zip(outs, widths)):
        if kv_transposed and idx in (1, 2):
            ref[...] = lax.dot_general(wkvt_ref[(idx - 1) * W_ATT:idx * W_ATT, :], h, NT_DIMS,
                                       preferred_element_type=F32)
        else:
            ref[...] = jnp.dot(h, w_ref[:, col:col + width], preferred_element_type=F32)
        col += width


def _mix_in(x, g, w_bf16, wkvt_bf16, tm, u_shape, u_spec, kv_shape, kv_spec):
    n = x.shape[0]
    rows = lambda w: pl.BlockSpec((tm, w), lambda i: (i, 0))
    conv_widths = (W_CONV, W_CONV, W_CONV)
    return pl.pallas_call(
        functools.partial(_mix_in_kernel, kv_transposed=len(kv_shape) == 3),
        grid=(n // tm,),
        in_specs=[rows(D_MODEL), _full((1, D_MODEL)), _full(w_bf16.shape), _full(wkvt_bf16.shape)],
        out_specs=[rows(W_ATT), kv_spec, kv_spec] + [rows(w) for w in conv_widths] + [u_spec],
        out_shape=[jax.ShapeDtypeStruct((n, W_ATT), F32), jax.ShapeDtypeStruct(kv_shape, F32),
                   jax.ShapeDtypeStruct(kv_shape, F32)]
                  + [jax.ShapeDtypeStruct((n, w), F32) for w in conv_widths] + [jax.ShapeDtypeStruct(u_shape, F32)],
        compiler_params=_cparams(("parallel",)),
    )(x, g, w_bf16, wkvt_bf16)


def _moba_prompt_kernel(slopes_ref, q_ref, kt_ref, vt_ref, o_ref, kb_sc, vb_sc, kmean_sc, *, n_blocks):
    j = pl.program_id(1)
    i = pl.program_id(2)
    blk = MOBA_BLOCK
    pair = 2 * HEAD_DIM

    @pl.when(i == 0)
    def _():
        lane = lax.broadcasted_iota(jnp.int32, (pair, LANES), 1)
        kmean = jnp.zeros((pair, LANES), F32)
        for n in range(n_blocks):
            kn = kt_ref[:, n * blk:(n + 1) * blk]
            kb_sc[n] = kn.astype(BF16)
            vb_sc[n] = vt_ref[:, n * blk:(n + 1) * blk].astype(BF16)
            kmean = jnp.where(lane == n, jnp.mean(kn, axis=1, keepdims=True), kmean)
        kmean_sc[...] = kmean

    q = q_ref[...]
    lane = lax.broadcasted_iota(jnp.int32, (blk, pair), 1)
    r_io = lax.broadcasted_iota(jnp.int32, (blk, blk), 0)
    c_io = lax.broadcasted_iota(jnp.int32, (blk, blk), 1)
    rel = (r_io - c_io).astype(F32)
    outs = []
    for hl in range(2):
        slope = slopes_ref[2 * j + hl]
        qh = jnp.where((lane >= HEAD_DIM * hl) & (lane < HEAD_DIM * (hl + 1)), q, 0.0)
        gate = jnp.dot(qh, kmean_sc[...], precision=HIGHEST, preferred_element_type=F32)
        is_past = lane < i
        keep = _top_k_mask(jnp.where(is_past, gate, -jnp.inf), n_blocks, MOBA_TOP_K) & is_past
        sel_bias = jnp.where(keep, 0.0, NEG)
        qb = (qh * ATT_SCALE).astype(BF16)

        s = jnp.dot(qb, kb_sc[i], preferred_element_type=F32) - slope * rel
        s = jnp.where(c_io <= r_io, s, NEG)
        m0 = jnp.max(s, axis=1, keepdims=True)
        e = jnp.exp(s - m0)
        l0 = jnp.sum(e, axis=1, keepdims=True)
        acc0 = lax.dot_general(e.astype(BF16), vb_sc[i], NT_DIMS, preferred_element_type=F32)

        def past_block(jj, carry, qb=qb, slope=slope, sel_bias=sel_bias):
            m_old, l_old, acc_old = carry
            dist = rel + ((i - jj) * blk).astype(F32)
            s = jnp.dot(qb, kb_sc[jj], preferred_element_type=F32) - slope * dist + _lane_column(sel_bias, jj)
            m_new = jnp.maximum(m_old, jnp.max(s, axis=1, keepdims=True))
            alpha = jnp.exp(m_old - m_new)
            e = jnp.exp(s - m_new)
            l_new = alpha * l_old + jnp.sum(e, axis=1, keepdims=True)
            acc_new = alpha * acc_old + lax.dot_general(e.astype(BF16), vb_sc[jj], NT_DIMS,
                                                        preferred_element_type=F32)
            return m_new, l_new, acc_new

        _, l_fin, acc_fin = lax.fori_loop(0, i, past_block, (m0, l0, acc0))
        outs.append(acc_fin / l_fin)
    o_ref[...] = jnp.where(lane < HEAD_DIM, outs[0], outs[1])


def _moba_prompt(q, kt, vt, slopes, bsz, seq):
    assert seq % MOBA_BLOCK == 0
    n_blocks = seq // MOBA_BLOCK
    assert n_blocks <= LANES
    pair = 2 * HEAD_DIM
    kv_spec = pl.BlockSpec((None, pair, seq), lambda b, j, i: (b, j, 0))
    qo_spec = pl.BlockSpec((None, MOBA_BLOCK, pair), lambda b, j, i: (b, i, j))
    out = pl.pallas_call(
        functools.partial(_moba_prompt_kernel, n_blocks=n_blocks),
        grid=(bsz, W_ATT // pair, n_blocks),
        in_specs=[pl.BlockSpec(memory_space=pltpu.SMEM), qo_spec, kv_spec, kv_spec],
        out_specs=qo_spec,
        out_shape=jax.ShapeDtypeStruct((bsz, seq, W_ATT), F32),
        scratch_shapes=[pltpu.VMEM((n_blocks, pair, MOBA_BLOCK), BF16), pltpu.VMEM((n_blocks, pair, MOBA_BLOCK), BF16),
                        pltpu.VMEM((pair, LANES), F32)],
        compiler_params=_cparams(("parallel", "parallel", "arbitrary")),
    )(slopes, q.reshape(bsz, seq, W_ATT), kt, vt)
    return out.reshape(bsz * seq, W_ATT)


PAGES_PER_STEP = 8


def _moba_sample_kernel(pt_ref, slope_ref, tq_ref, q_ref, kn_ref, vn_ref, *rest, n_pages, page, t_new):
    del pt_ref
    k_refs = rest[:PAGES_PER_STEP]
    v_refs = rest[PAGES_PER_STEP:2 * PAGES_PER_STEP]
    o_ref, st_sc, ksum_sc, qbd_sc, qbdf_sc, own_sc, inv_sc, acc_sc = rest[2 * PAGES_PER_STEP:]
    ph = pl.program_id(1)
    pg = pl.program_id(2)
    n_groups = n_pages // PAGES_PER_STEP
    ppb = MOBA_BLOCK // page
    n_blocks = n_pages // ppb
    past = n_pages * page
    ht = N_HEADS * t_new

    @pl.when((ph == 0) & (pg == 0))
    def _():
        q = q_ref[...]
        qt = jnp.concatenate([q] * N_HEADS + [jnp.zeros((LANES - ht, W_ATT), F32)], axis=0)
        row = lax.broadcasted_iota(jnp.int32, (LANES, W_ATT), 0)
        col = lax.broadcasted_iota(jnp.int32, (LANES, W_ATT), 1)
        qbd = jnp.where((col // HEAD_DIM == row // t_new) & (row < ht), qt, 0.0)
        qbdf_sc[...] = qbd
        qbd_sc[...] = (qbd * ATT_SCALE).astype(BF16)
        ksum_sc[...] = jnp.zeros_like(ksum_sc)

    @pl.when(ph == 0)
    def _():
        lane = lax.broadcasted_iota(jnp.int32, (W_ATT, LANES), 1)
        for ii in range(PAGES_PER_STEP):
            kp = k_refs[ii][...]
            pidx = pg * PAGES_PER_STEP + ii
            ksum_sc[...] += jnp.where(lane == pidx // ppb, jnp.sum(kp, axis=1, keepdims=True), 0.0)
            st_sc[pidx] = jnp.dot(qbd_sc[...], kp.astype(BF16), preferred_element_type=F32)

    @pl.when((ph == 0) & (pg == n_groups - 1))
    def _():
        slope = slope_ref[...]
        tq = tq_ref[...]
        lane = lax.broadcasted_iota(jnp.int32, (LANES, LANES), 1)
        lane_f = lane.astype(F32)
        gate = jnp.dot(qbdf_sc[...], ksum_sc[...] * (1.0 / MOBA_BLOCK), precision=HIGHEST,
                       preferred_element_type=F32)
        keep = _top_k_mask(jnp.where(lane < n_blocks, gate, -jnp.inf), n_blocks, min(MOBA_TOP_K, n_blocks))
        sel_bias = jnp.where(keep & (lane < n_blocks), 0.0, NEG)

        k_own = jnp.concatenate([kn_ref[...], jnp.zeros((LANES - t_new, W_ATT), F32)], axis=0)
        so = lax.dot_general(qbd_sc[...], k_own.astype(BF16), NT_DIMS, preferred_element_type=F32)
        lo =jnp.where((lane_f <= tq) & (lane < t_new), so - slope * (tq - lane_f), NEG)
        qpos = tq + float(past)

        def logits(pp):
            dist = qpos - (lane_f + (pp * page).astype(F32))
            return st_sc[pp] - slope * dist + _lane_column(sel_bias, pp // ppb)

        m_vec = lax.fori_loop(0, n_pages, lambda pp, mv: jnp.maximum(mv, logits(pp)), lo)
        m = jnp.max(m_vec, axis=1, keepdims=True)
        eo = jnp.exp(lo - m)

        def exp_body(pp, l_vec):
            e = jnp.exp(logits(pp) - m)
            st_sc[pp] = e
            return l_vec + e

        l_vec = lax.fori_loop(0, n_pages, exp_body, eo)
        inv = 1.0 / jnp.sum(l_vec, axis=1, keepdims=True)
        inv_sc[...] = jnp.broadcast_to(inv, inv_sc.shape)
        own_sc[...] = eo * inv
        acc_sc[...] = jnp.zeros_like(acc_sc)

    @pl.when(ph == 1)
    def _():
        inv = inv_sc[...]
        for ii in range(PAGES_PER_STEP):
            pidx = pg * PAGES_PER_STEP + ii
            p = (st_sc[pidx] * inv).astype(BF16)
            acc_sc[...] += lax.dot_general(p, v_refs[ii][...].astype(BF16), NT_DIMS, preferred_element_type=F32)

    @pl.when((ph == 1) & (pg == n_groups - 1))
    def _():
        v_own = jnp.concatenate([vn_ref[...], jnp.zeros((LANES - t_new, W_ATT), F32)], axis=0)
        acc = acc_sc[...] + jnp.dot(own_sc[...].astype(BF16), v_own.astype(BF16), preferred_element_type=F32)
        col = lax.broadcasted_iota(jnp.int32, (t_new, W_ATT), 1)
        out = jnp.zeros((t_new, W_ATT), F32)
        for h in range(N_HEADS):
            out += jnp.where(col // HEAD_DIM == h, acc[h * t_new:(h + 1) * t_new, :], 0.0)
        o_ref[...] = out


def _moba_sample(q, k, v, cache_kt, cache_vt, layer, page_table, slopes_np, bsz, t_new):
    page = cache_kt.shape[-1]
    n_pages = page_table.shape[1]
    assert MOBA_BLOCK % page == 0 and (n_pages * page) % MOBA_BLOCK == 0 and n_pages % PAGES_PER_STEP == 0
    assert N_HEADS * t_new <= LANES and t_new % SUBLANES == 0 and page == LANES
    n_groups = n_pages // PAGES_PER_STEP
    n_blocks = n_pages * page // MOBA_BLOCK
    assert n_blocks <= LANES
    row_h = np.minimum(np.arange(LANES) // t_new, N_HEADS - 1)
    slope_r = jnp.asarray(np.broadcast_to(slopes_np[row_h][:, None], (LANES, LANES)), F32)
    tq_r = jnp.asarray(np.broadcast_to((np.arange(LANES) % t_new)[:, None], (LANES, LANES)), F32)
    wide = lambda a: a.reshape(t_new, bsz * W_ATT)
    tok_spec = pl.BlockSpec((t_new, W_ATT), lambda b, ph, pg, pt: (0, b))
    const_spec = pl.BlockSpec((LANES, LANES), lambda b, ph, pg, pt: (0, 0))

    def k_map(ii):
        def f(b, ph, pg, pt):
            p = jnp.where(ph == 0, pg, n_groups - 1) * PAGES_PER_STEP + ii
            return (layer, pt[b * n_pages + p], 0, 0)
        return f

    def v_map(ii):
        def f(b, ph, pg, pt):
            p = jnp.where(ph == 1, pg, 0) * PAGES_PER_STEP + ii
            return (layer, pt[b * n_pages + p], 0, 0)
        return f

    page_block = (None, None, W_ATT, page)
    grid_spec = pltpu.PrefetchScalarGridSpec(
        num_scalar_prefetch=1,
        grid=(bsz, 2, n_groups),
        in_specs=[const_spec, const_spec, tok_spec, tok_spec, tok_spec]
                 + [pl.BlockSpec(page_block, k_map(ii)) for ii in range(PAGES_PER_STEP)]
                 + [pl.BlockSpec(page_block, v_map(ii)) for ii in range(PAGES_PER_STEP)],
        out_specs=tok_spec,
        scratch_shapes=[pltpu.VMEM((n_pages, LANES, page), F32),
                        pltpu.VMEM((W_ATT, LANES), F32),
                        pltpu.VMEM((LANES, W_ATT), BF16),
                        pltpu.VMEM((LANES, W_ATT), F32),
                        pltpu.VMEM((LANES, LANES), F32),
                        pltpu.VMEM((LANES, LANES), F32),
                        pltpu.VMEM((LANES, W_ATT), F32)],
    )
    out = pl.pallas_call(
        functools.partial(_moba_sample_kernel, n_pages=n_pages, page=page, t_new=t_new),
        grid_spec=grid_spec,
        out_shape=jax.ShapeDtypeStruct((t_new, bsz * W_ATT), F32),
        compiler_params=_cparams(("parallel", "arbitrary", "arbitrary")),
    )(page_table.reshape(-1), slope_r, tq_r, wide(q), wide(k), wide(v), *([cache_kt] * PAGES_PER_STEP),
      *([cache_vt] * PAGES_PER_STEP))
    return out.reshape(t_new * bsz, W_ATT)


SCAN_LANES = 256


def _gelu_tanh(x):
    return 0.5 * x * (1.0 + jnp.tanh(np.float32(np.sqrt(2.0 / np.pi)) * (x + 0.044715 * (x * x * x))))


def _ssm_kernel(u_ref, h0r_ref, h0i_ref, a_ref, bbd_ref, cbd_ref, d_ref, wglu_ref,
                y_ref, hr_ref, hi_ref, xs_sc, hr_sc, hi_sc, *, bsz, t_chunk):
    step = pl.program_id(0)

    @pl.when(step == 0)
    def _():
        hr_sc[...] = h0r_ref[...]
        hi_sc[...] = h0i_ref[...]

    u = u_ref[...]
    xs_sc[...] = jnp.dot(u.astype(BF16), bbd_ref[...], preferred_element_type=F32)
    for lc in range(N_STATE // SCAN_LANES):
        re = slice(lc * SCAN_LANES, (lc + 1) * SCAN_LANES)
        im = slice(N_STATE + lc * SCAN_LANES, N_STATE + (lc + 1) * SCAN_LANES)
        ar = jnp.broadcast_to(a_ref[0:1, re], (bsz, SCAN_LANES))
        ai = jnp.broadcast_to(a_ref[1:2, re], (bsz, SCAN_LANES))

        def scan_step(t, carry, re=re, im=im, ar=ar, ai=ai):
            hr, hi = carry
            rows = pl.ds(pl.multiple_of(t * bsz, bsz), bsz)
            nr = ar * hr - ai * hi + xs_sc[rows, re]
            ni = ar * hi + ai * hr + xs_sc[rows, im]
            xs_sc[rows, re] = nr
            xs_sc[rows, im] = ni
            return nr, ni

        hr, hi = lax.fori_loop(0, t_chunk, scan_step, (hr_sc[:, re], hi_sc[:, re]))
        hr_sc[:, re] = hr
        hi_sc[:, re] = hi

    y = jnp.dot(xs_sc[...].astype(BF16), cbd_ref[...], preferred_element_type=F32) + d_ref[...] * u
    y = _gelu_tanh(y)
    y_ref[...] = y * jax.nn.sigmoid(jnp.dot(y.astype(BF16), wglu_ref[...], preferred_element_type=F32))

    @pl.when(step == pl.num_programs(0) - 1)
    def _():
        hr_ref[...] = hr_sc[...]
        hi_ref[...] = hi_sc[...]


def _ssm(u_tb, h0r, h0i, a, bbd, cbd, d, wglu, bsz, seq, t_chunk):
    assert bsz % SUBLANES == 0 and seq % t_chunk == 0
    rows = t_chunk * bsz
    blk = pl.BlockSpec((rows, W_SSM), lambda s: (s, 0))
    st_spec = _full((bsz, N_STATE))
    return pl.pallas_call(
        functools.partial(_ssm_kernel, bsz=bsz, t_chunk=t_chunk),
        grid=(seq // t_chunk,),
        in_specs=[blk, st_spec, st_spec, _full(a.shape), _full(bbd.shape), _full(cbd.shape), _full(d.shape),
                  _full(wglu.shape)],
        out_specs=[blk, st_spec, st_spec],
        out_shape=[jax.ShapeDtypeStruct((seq * bsz, W_SSM), F32), jax.ShapeDtypeStruct((bsz, N_STATE), F32),
                   jax.ShapeDtypeStruct((bsz, N_STATE), F32)],
        scratch_shapes=[pltpu.VMEM((rows, 2 * N_STATE), F32), pltpu.VMEM((bsz, N_STATE), F32),
                        pltpu.VMEM((bsz, N_STATE), F32)],
        compiler_params=_cparams(("arbitrary",)),
    )(u_tb, h0r, h0i, a, bbd, cbd, d, wglu)


def _mix_out_kernel(*refs, tm, halo, row_stride, tiles_per_seq):
    if halo:
        (att_ref, cb_ref, cc_ref, cx_ref, cch_ref, cxh_ref, zst_ref, y_ref, x_ref, gb_ref, wout_ref, cw_ref,
         cbias_ref, xo_ref, zlast_ref, zs_sc) = refs
    else:
        (att_ref, cb_ref, cc_ref, cx_ref, zst_ref, y_ref, x_ref, gb_ref, wout_ref, cw_ref,
         cbias_ref, xo_ref, zlast_ref, zs_sc) = refs
    pre = zs_sc.shape[0] - tm
    z = cc_ref[...] * cx_ref[...]
    if halo:
        first = pl.program_id(0) % tiles_per_seq == 0
        zs_sc[0:pre, :] = jnp.where(first, zst_ref[...], cch_ref[...] * cxh_ref[...])
    else:
        zs_sc[0:pre, :] = zst_ref[...]
    zs_sc[pre:, :] = z
    yc = cbias_ref[...]
    for tap in range(CONV_WIDTH - 1):
        off = pre - (CONV_WIDTH - 1 - tap) * row_stride
        yc = yc + cw_ref[tap:tap + 1, :] * zs_sc[off:off + tm, :]
    yc = yc + cw_ref[CONV_WIDTH - 1:CONV_WIDTH, :] * z
    conv_out = cb_ref[...] * yc
    zlast_ref[...] = zs_sc[tm:tm + pre, :]

    acc = x_ref[...]
    col = 0
    for branch in (att_ref[...], conv_out, y_ref[...]):
        width = branch.shape[-1]
        nb = (_rms(branch) * gb_ref[:, col:col + width]).astype(BF16)
        acc = acc + jnp.dot(nb, wout_ref[col:col + width, :], preferred_element_type=F32)
        col += width
    xo_ref[...] = acc


def _mix_out(att, cb, cc, cx, zstate, y_arr, y_spec, x, gb, wout, cw, cbias, tm, halo, row_stride, tiles_per_seq,
             zstate_spec):
    n = x.shape[0]
    pre = zstate_spec.block_shape[-2]
    rows = lambda w: pl.BlockSpec((tm, w), lambda i: (i, 0))
    in_specs = [rows(W_ATT), rows(W_CONV), rows(W_CONV), rows(W_CONV)]
    args = [att, cb, cc, cx]
    if halo:
        per = tm // SUBLANES
        halo_spec = pl.BlockSpec((SUBLANES, W_CONV), lambda i: (jnp.maximum(i * per - 1, 0), 0))
        in_specs += [halo_spec, halo_spec]
        args += [cc, cx]
    in_specs += [zstate_spec, y_spec, rows(D_MODEL), _full(gb.shape), _full(wout.shape), _full(cw.shape),
                 _full(cbias.shape)]
    args += [zstate, y_arr, x, gb, wout, cw, cbias]
    n_tiles = n // tm
    return pl.pallas_call(
        functools.partial(_mix_out_kernel, tm=tm, halo=halo, row_stride=row_stride, tiles_per_seq=tiles_per_seq),
        grid=(n_tiles,),
        in_specs=in_specs,
        out_specs=[rows(D_MODEL), pl.BlockSpec((pre, W_CONV), lambda i: (i, 0))],
        out_shape=[jax.ShapeDtypeStruct((n, D_MODEL), F32), jax.ShapeDtypeStruct((n_tiles * pre, W_CONV), F32)],
        scratch_shapes=[pltpu.VMEM((tm + pre, W_CONV), F32)],
        compiler_params=_cparams(("parallel",)),
    )(*args)


def _ffn_kernel(*refs, moe, final):
    refs = list(refs)
    x_ref, g_ref = refs[:2]
    pos = 2
    if moe:
        wr_ref, br_ref = refs[pos:pos + 2]
        pos += 2
    wg_ref, wu_ref, wd_ref = refs[pos:pos + 3]
    pos += 3
    if final:
        gf_ref = refs[pos]
        pos += 1
    o_ref, h_sc, acc_sc = refs[pos:pos + 3]
    pos += 3
    if moe:
        comb_sc = refs[pos]
    e = pl.program_id(1)
    c = pl.program_id(2)
    last = (e == pl.num_programs(1) - 1) & (c == pl.num_programs(2) - 1)

    @pl.when((e == 0) & (c == 0))
    def _():
        h = _rms(x_ref[...]) * g_ref[...]
        h_sc[...] = h.astype(BF16)
        acc_sc[...] = jnp.zeros_like(acc_sc)
        if moe:
            logits = jnp.dot(h, wr_ref[...], precision=HIGHEST, preferred_element_type=F32) + br_ref[...]
            lane = lax.broadcasted_iota(jnp.int32, logits.shape, 1)
            logits = jnp.where(lane < N_EXPERTS, logits, -jnp.inf)
            keep = _top_k_mask(logits, N_EXPERTS, EXPERT_TOP_K) & (lane < N_EXPERTS)
            top = jnp.max(logits, axis=1, keepdims=True)
            w = jnp.where(keep, jnp.exp(logits - top), 0.0)
            comb_sc[...] = w / jnp.sum(w, axis=1, keepdims=True)

    hb = h_sc[...]
    a = jnp.dot(hb, wg_ref[...], preferred_element_type=F32)
    b = jnp.dot(hb, wu_ref[...], preferred_element_type=F32)
    t = (a * jax.nn.sigmoid(a) * b).astype(BF16)
    d = jnp.dot(t, wd_ref[...], preferred_element_type=F32)
    if moe:
        d = d * _lane_column(comb_sc[...], e)
    acc_sc[...] += d

    @pl.when(last)
    def _():
        out = x_ref[...] + acc_sc[...]
        if final:
            out = _rms(out) * gf_ref[...]
        o_ref[...] = out


def _ffn(x, g, wg, wu, wd, tm, f_chunk, router=None, g_final=None):
    n = x.shape[0]
    n_exp, _, f_dim = wg.shape
    moe = router is not None
    final = g_final is not None
    rows = pl.BlockSpec((tm, D_MODEL), lambda i, e, c: (i, 0))
    const = lambda shape: pl.BlockSpec(shape, lambda i, e, c: (0,) * len(shape))
    in_specs = [rows, const((1, D_MODEL))]
    args = [x, g]
    if moe:
        in_specs += [const(router[0].shape), const(router[1].shape)]
        args += list(router)
    in_specs += [pl.BlockSpec((None, D_MODEL, f_chunk), lambda i, e, c: (e, 0, c)),
                 pl.BlockSpec((None, D_MODEL, f_chunk), lambda i, e, c: (e, 0, c)),
                 pl.BlockSpec((None, f_chunk, D_MODEL), lambda i, e, c: (e, c, 0))]
    args += [wg, wu, wd]
    if final:
        in_specs.append(const((1, D_MODEL)))
        args.append(g_final)
    scratch = [pltpu.VMEM((tm, D_MODEL), BF16), pltpu.VMEM((tm, D_MODEL), F32)]
    if moe:
        scratch.append(pltpu.VMEM((tm, LANES), F32))
    return pl.pallas_call(
        functools.partial(_ffn_kernel, moe=moe, final=final),
        grid=(n // tm, n_exp, f_dim // f_chunk),
        in_specs=in_specs,
        out_specs=rows,
        out_shape=jax.ShapeDtypeStruct((n, D_MODEL), F32),
        scratch_shapes=scratch,
        compiler_params=_cparams(("parallel", "arbitrary", "arbitrary")),
    )(*args)


def _alibi_slopes_np():
    return (2.0 ** (-8.0 * np.arange(1, N_HEADS + 1) / N_HEADS)).astype(np.float32)


def _ssm_params(a_re, a_im, log_dt, b_re, b_im, c_re, c_im):
    ar, ai = a_re.astype(F32), a_im.astype(F32)
    dt = jnp.exp(log_dt.astype(F32))[:, None]
    mag = jnp.exp(dt * ar)
    abr, abi = mag * jnp.cos(dt * ai), mag * jnp.sin(dt * ai)
    den = ar * ar + ai * ai
    zr = ((abr - 1.0) * ar + abi * ai) / den
    zi = (abi * ar - (abr - 1.0) * ai) / den
    br, bim = b_re.astype(F32), b_im.astype(F32)
    bbr = zr[..., None] * br - zi[..., None] * bim
    bbi = zr[..., None] * bim + zi[..., None] * br
    eye = jnp.eye(N_GROUPS, dtype=F32)
    to_in = lambda m: jnp.einsum('gnc,gh->gchn', m, eye).reshape(W_SSM, N_STATE)
    to_out = lambda m: jnp.einsum('gcn,gh->gnhc', m, eye).reshape(N_STATE, W_SSM)
    bbd = jnp.concatenate([to_in(bbr), to_in(bbi)], axis=1).astype(BF16)
    cbd = jnp.concatenate([to_out(c_re.astype(F32)), -to_out(c_im.astype(F32))], axis=0).astype(BF16)
    a = jnp.stack([abr.reshape(N_STATE), abi.reshape(N_STATE)])
    return a, bbd, cbd


def _trunk(x, w, depth, bsz, seq, time_major, attend, conv0, h0r, h0i):
    n = bsz * seq
    ks, vs, zs, hrs, his = [], [], [], [], []
    if time_major:
        tm, tiles_per_seq, row_stride, halo = n, 1, bsz, False
        pre = (CONV_WIDTH - 1) * bsz
        u_shape = (n, W_SSM)
        u_spec = pl.BlockSpec((tm, W_SSM), lambda i: (0, 0))
        kv_shape = (n, W_ATT)
        kv_spec = pl.BlockSpec((tm, W_ATT), lambda i: (0, 0))
        zstate_spec = pl.BlockSpec((pre, W_CONV), lambda i: (0, 0))
        t_chunk = seq
    else:
        tm, row_stride, halo = ROW_TILE, 1, True
        tiles_per_seq = seq // tm
        pre = SUBLANES
        u_shape = (seq, bsz * W_SSM)
        u_spec = pl.BlockSpec((tm, W_SSM), lambda i: (i % tiles_per_seq, i // tiles_per_seq))
        kv_shape = (bsz, W_ATT, seq)
        kv_spec = pl.BlockSpec((None, W_ATT, tm), lambda i: (i // tiles_per_seq, 0, i % tiles_per_seq))
        zstate_spec = pl.BlockSpec((None, pre, W_CONV), lambda i: (i // tiles_per_seq, 0, 0))
        t_chunk = 32
    for l in range(depth):
        q, k, v, cb, cc, cx, u = _mix_in(x, w['g_mix'][l], w['w_in'][l], w['w_kvt'][l], tm, u_shape, u_spec,
                                         kv_shape, kv_spec)
        att = attend(l, q, k, v)
        y, hr, hi = _ssm(u.reshape(seq * bsz, W_SSM), h0r[l], h0i[l], *w['ssm'][l], bsz, seq, t_chunk)
        x, zlast = _mix_out(att, cb, cc, cx, conv0[l], y.reshape(u_shape), u_spec, x, w['g_branch'][l], w['w_out'][l],
                            w['conv_w'][l], w['conv_b'][l], tm, halo, row_stride, tiles_per_seq, zstate_spec)
        g_final = w['g_final'] if l == depth - 1 else None
        if l % 2 == 0:
            x = _ffn(x, w['g_ffn'][l], *w['dense'][l // 2], tm, w['dense'][l // 2][0].shape[-1] // 2, g_final=g_final)
        else:
            wg, wu, wd, router = w['moe'][l // 2]
            x = _ffn(x, w['g_ffn'][l], wg, wu, wd, tm, wg.shape[-1], router=router, g_final=g_final)
        ks.append(k)
        vs.append(v)
        zs.append(zlast)
        hrs.append(hr)
        his.append(hi)
    return x, ks, vs, zs, hrs, his


def kernel(x_prompt, x_sample, cache_k, cache_v, state_conv, state_ssm_re, state_ssm_im, page_table, g_mix_norm, w_in, g_branch, w_out, conv_w, conv_b, ssm_a_re, ssm_a_im, ssm_log_dt, ssm_b_re, ssm_b_im, ssm_c_re, ssm_c_im, ssm_d, ssm_w_glu, g_ffn_norm, w_ffn_gate, w_ffn_up, w_ffn_down, w_router, b_router, w_exp_gate, w_exp_up, w_exp_down, g_final):
    depth = w_in.shape[0]
    bp, sp, _ = x_prompt.shape
    bs, ss, _ = x_sample.shape
    n_pool, page = cache_k.shape[1], cache_k.shape[2]
    slopes_np = _alibi_slopes_np()
    slopes = jnp.asarray(slopes_np)

    row = lambda a: a.reshape(a.shape[0], 1, a.shape[-1])
    w_in_bf16 = w_in.astype(BF16)
    w = {
        'g_mix': row(g_mix_norm), 'w_in': w_in_bf16, 'g_branch': row(g_branch), 'w_out': w_out.astype(BF16),
        'w_kvt': jnp.swapaxes(w_in_bf16[:, :, W_ATT:3 * W_ATT], 1, 2),
        'conv_w': conv_w, 'conv_b': row(conv_b), 'g_ffn': row(g_ffn_norm), 'g_final': g_final.reshape(1, D_MODEL),
        'ssm': [(*_ssm_params(ssm_a_re[l], ssm_a_im[l], ssm_log_dt[l], ssm_b_re[l], ssm_b_im[l], ssm_c_re[l],
                              ssm_c_im[l]), ssm_d[l].reshape(1, W_SSM), ssm_w_glu[l].astype(BF16))
                for l in range(depth)],
        'dense': [(w_ffn_gate[j:j + 1].astype(BF16), w_ffn_up[j:j + 1].astype(BF16), w_ffn_down[j:j + 1].astype(BF16))
                  for j in range(w_ffn_gate.shape[0])],
        'moe': [(w_exp_gate[j].astype(BF16), w_exp_up[j].astype(BF16), w_exp_down[j].astype(BF16),
                 (jnp.pad(w_router[j], ((0, 0), (0, LANES - N_EXPERTS))),
                  jnp.pad(b_router[j].reshape(1, N_EXPERTS), ((0, 0), (0, LANES - N_EXPERTS)))))
                for j in range(w_exp_gate.shape[0])],
    }

    conv0_p = jnp.zeros((depth, bp, SUBLANES, W_CONV), F32)
    h0_p = jnp.zeros((depth, bp, N_STATE), F32)
    attend_p = lambda l, q, kt, vt: _moba_prompt(q, kt, vt, slopes, bp, sp)
    yp, kts, vts, zs, hrs, his = _trunk(x_prompt.reshape(bp * sp, D_MODEL), w, depth, bp, sp, False, attend_p,
                                        conv0_p, h0_p, h0_p)
    y_prompt = yp.reshape(bp, sp, D_MODEL)
    from_t = lambda a: jnp.transpose(jnp.stack(a).reshape(depth, bp, N_HEADS, HEAD_DIM, sp), (0, 1, 4, 2, 3))
    k_p, v_p = from_t(kts), from_t(vts)
    tiles = sp // ROW_TILE
    conv_p = jnp.stack(zs).reshape(depth, bp, tiles, SUBLANES, W_CONV)[:, :, -1, SUBLANES - (CONV_WIDTH - 1):, :]
    sre_p = jnp.stack(hrs).reshape(depth, bp, N_GROUPS, SSM_STATE)
    sim_p = jnp.stack(his).reshape(depth, bp, N_GROUPS, SSM_STATE)

    to_t = lambda c: jnp.transpose(c, (0, 1, 3, 4, 2)).reshape(depth, n_pool, W_ATT, page)
    cache_kt, cache_vt = to_t(cache_k), to_t(cache_v)
    xs = jnp.swapaxes(x_sample, 0, 1).reshape(ss * bs, D_MODEL)
    conv0_s = jnp.swapaxes(state_conv, 1, 2).reshape(depth, (CONV_WIDTH - 1) * bs, W_CONV)
    h0r_s = state_ssm_re.reshape(depth, bs, N_STATE)
    h0i_s = state_ssm_im.reshape(depth, bs, N_STATE)
    attend_s = lambda l, q, k, v: _moba_sample(q, k, v, cache_kt, cache_vt, l, page_table, slopes_np, bs, ss)
    ys, ks, vs, zs, hrs, his = _trunk(xs, w, depth, bs, ss, True, attend_s, conv0_s, h0r_s, h0i_s)
    from_tb = lambda a, width: jnp.swapaxes(a.reshape(-1, ss, bs, width), 1, 2)
    y_sample = from_tb(ys, D_MODEL)[0]
    k_s = from_tb(jnp.stack(ks), W_ATT).reshape(depth, bs, ss, N_HEADS, HEAD_DIM)
    v_s = from_tb(jnp.stack(vs), W_ATT).reshape(depth, bs, ss, N_HEADS, HEAD_DIM)
    conv_s = jnp.swapaxes(jnp.stack(zs).reshape(depth, CONV_WIDTH - 1, bs, W_CONV), 1, 2)
    sre_s = jnp.stack(hrs).reshape(depth, bs, N_GROUPS, SSM_STATE)
    sim_s = jnp.stack(his).reshape(depth, bs, N_GROUPS, SSM_STATE)
    return (y_prompt, y_sample, k_p, v_p, conv_p, sre_p, sim_p, k_s, v_s, conv_s, sre_s, sim_s)
```

```python
import functools

import numpy as np
import jax
import jax.numpy as jnp
from jax import lax
from jax.experimental import pallas as pl
from jax.experimental.pallas import tpu as pltpu

F32 = jnp.float32
BF16 = jnp.bfloat16
HIGHEST = lax.Precision.HIGHEST

D_MODEL = 1024
N_HEADS = 8
HEAD_DIM = 64
W_ATT = N_HEADS * HEAD_DIM
W_CONV = 256
W_SSM = 256
CONV_WIDTH = 3
SSM_GROUP = 16
N_GROUPS = 16
SSM_STATE = 64
N_STATE = N_GROUPS * SSM_STATE
MOBA_BLOCK = 256
MOBA_TOP_K = 3
ATT_SCALE = HEAD_DIM ** -0.5
N_EXPERTS = 8
EXPERT_TOP_K = 2
RMS_EPS = 1e-6
NEG = -1e30

LANES = 128
SUBLANES = 8
VMEM_LIMIT = 56 * 1024 * 1024
ROW_TILE = 512
NT_DIMS = (((1,), (1,)), ((), ()))


def _cparams(sem):
    return pltpu.CompilerParams(dimension_semantics=sem, vmem_limit_bytes=VMEM_LIMIT)


def _rms(x):
    return x * lax.rsqrt(jnp.mean(x * x, axis=-1, keepdims=True) + RMS_EPS)


def _full(shape):
    n = len(shape)
    return pl.BlockSpec(shape, lambda *_: (0,) * n)


def _lane_column(x, idx):
    lane = lax.broadcasted_iota(jnp.int32, x.shape, 1)
    return jnp.sum(jnp.where(lane == idx, x, 0.0), axis=1, keepdims=True)


def _top_k_mask(score, first, n_cand, k):
    lane = lax.broadcasted_iota(jnp.int32, score.shape, 1)
    rank = jnp.zeros(score.shape, jnp.int32)
    for m in range(first, first + n_cand):
        sm = score[:, m:m + 1]
        rank += ((sm > score) | ((sm == score) & (m < lane))).astype(jnp.int32)
    return rank < k


def _mix_in_kernel(x_ref, g_ref, w_ref, wkvt_ref, q_ref, k_ref, v_ref, cb_ref, cc_ref, cx_ref, u_ref, *, kv_transposed):
    h = (_rms(x_ref[...]) * g_ref[...]).astype(BF16)
    outs = (q_ref, k_ref, v_ref, cb_ref, cc_ref, cx_ref, u_ref)
    widths = (W_ATT,) * 3 + (W_CONV,) * 3 + (W_SSM,)
    col = 0
    for idx, (ref, width) in enumerate(zip(outs, widths)):
        if kv_transposed and idx in (1, 2):
            ref[...] = lax.dot_general(wkvt_ref[(idx - 1) * W_ATT:idx * W_ATT, :], h, NT_DIMS,
                                       preferred_element_type=F32)
        else:
            ref[...] = jnp.dot(h, w_ref[:, col:col + width], preferred_element_type=F32)
        col += width


def _mix_in(x, g, w_bf16, wkvt_bf16, tm, u_shape, u_spec, kv_shape, kv_spec):
    n = x.shape[0]
    rows = lambda w: pl.BlockSpec((tm, w), lambda i: (i, 0))
    conv_widths = (W_CONV, W_CONV, W_CONV)
    return pl.pallas_call(
        functools.partial(_mix_in_kernel, kv_transposed=len(kv_shape) == 3),
        grid=(n // tm,),
        in_specs=[rows(D_MODEL), _full((1, D_MODEL)), _full(w_bf16.shape), _full(wkvt_bf16.shape)],
        out_specs=[rows(W_ATT), kv_spec, kv_spec] + [rows(w) for w in conv_widths] + [u_spec],
        out_shape=[jax.ShapeDtypeStruct((n, W_ATT), F32), jax.ShapeDtypeStruct(kv_shape, F32),
                   jax.ShapeDtypeStruct(kv_shape, F32)]
                  + [jax.ShapeDtypeStruct((n, w), F32) for w in conv_widths] + [jax.ShapeDtypeStruct(u_shape, F32)],
        compiler_params=_cparams(("parallel",)),
    )(x, g, w_bf16, wkvt_bf16)


AUG_KPOS = 32


def _moba_prompt_consts(seq):
    n_blocks = seq // MOBA_BLOCK
    kc = np.zeros((2, 2 * HEAD_DIM, seq), np.float32)
    vc = np.zeros((2, 2 * HEAD_DIM, seq), np.float32)
    for hl in range(2):
        aug = HEAD_DIM * (1 - hl)
        for n in range(n_blocks):
            kc[hl, aug + n, n * MOBA_BLOCK:(n + 1) * MOBA_BLOCK] = 1.0
        kc[hl, aug + AUG_KPOS, :] = np.arange(seq) % MOBA_BLOCK
        vc[hl, aug, :] = 1.0
    return jnp.asarray(kc), jnp.asarray(vc)


def _moba_prompt_kernel(slopes_ref, q_ref, kt_ref, vt_ref, kc_ref, vc_ref, o_ref, kb_sc, vb_sc, *, n_blocks):
    j = pl.program_id(1)
    blk = MOBA_BLOCK
    pair = 2 * HEAD_DIM
    kt = kt_ref[...]
    vt = vt_ref[...]
    row = lax.broadcasted_iota(jnp.int32, kt.shape, 0)
    for hl in range(2):
        own_rows = (row >= HEAD_DIM * hl) & (row < HEAD_DIM * (hl + 1))
        kb_sc[hl] = jnp.where(own_rows, kt, kc_ref[hl]).astype(BF16)
        vb_sc[hl] = jnp.where(own_rows, vt, vc_ref[hl]).astype(BF16)

    krow = lax.broadcasted_iota(jnp.int32, (pair, LANES), 0)
    klane = lax.broadcasted_iota(jnp.int32, (pair, LANES), 1)
    aug_of_row = jnp.where(krow < HEAD_DIM, HEAD_DIM, 0)
    kmean = jnp.zeros((pair, LANES), F32)
    for n in range(n_blocks):
        mean_n = jnp.mean(kt[:, n * blk:(n + 1) * blk], axis=1, keepdims=True)
        kmean = jnp.where(klane == aug_of_row + n, mean_n, kmean)

    lane = lax.broadcasted_iota(jnp.int32, (blk, pair), 1)
    r_io = lax.broadcasted_iota(jnp.int32, (blk, blk), 0)
    c_io = lax.broadcasted_iota(jnp.int32, (blk, blk), 1)
    causal = c_io <= r_io
    for i in range(n_blocks):
        q_i = q_ref[i * blk:(i + 1) * blk, :]
        past, width = i * blk, (i + 1) * blk
        outs = []
        for hl in range(2):
            aug = HEAD_DIM * (1 - hl)
            slope = slopes_ref[2 * j + hl]
            qh = jnp.where((lane >= HEAD_DIM * hl) & (lane < HEAD_DIM * (hl + 1)), q_i, 0.0)
            is_past = (lane >= aug) & (lane < aug + i)
            if i > MOBA_TOP_K:
                gate = jnp.dot(qh, kmean, precision=HIGHEST, preferred_element_type=F32)
                keep = _top_k_mask(jnp.where(is_past, gate, -jnp.inf), aug, i, MOBA_TOP_K) & is_past
            else:
                keep = is_past
            bias = jnp.where(keep | (lane == aug + i), 0.0, NEG) + slope * (blk * (lane - (aug + i)).astype(F32))
            q_aug = jnp.where((lane >= aug) & (lane <= aug + i), bias, qh * ATT_SCALE)
            q_aug = jnp.where(lane == aug + AUG_KPOS, slope, q_aug).astype(BF16)

            s = jnp.dot(q_aug, kb_sc[hl, :, 0:width], preferred_element_type=F32)
            s_own = jnp.where(causal, s[:, past:width], NEG)
            m = jnp.max(s_own, axis=1, keepdims=True)
            if i > 0:
                m = jnp.maximum(m, jnp.max(s[:, 0:past], axis=1, keepdims=True))
            o = lax.dot_general(jnp.exp(s_own - m).astype(BF16), vb_sc[hl, :, past:width], NT_DIMS,
                                preferred_element_type=F32)
            if i > 0:
                o = o + lax.dot_general(jnp.exp(s[:, 0:past] - m).astype(BF16), vb_sc[hl, :, 0:past], NT_DIMS,
                                        preferred_element_type=F32)
            outs.append(o * (1.0 / o[:, aug:aug + 1]))
        o_ref[i * blk:(i + 1) * blk, :] = jnp.where(lane < HEAD_DIM, outs[0], outs[1])


def _moba_prompt(q, kt, vt, slopes, bsz, seq):
    assert seq % MOBA_BLOCK == 0
    n_blocks = seq // MOBA_BLOCK
    assert n_blocks <= AUG_KPOS < HEAD_DIM
    pair = 2 * HEAD_DIM
    kc, vc = _moba_prompt_consts(seq)
    kv_spec = pl.BlockSpec((None, pair, seq), lambda b, j: (b, j, 0))
    qo_spec = pl.BlockSpec((None, seq, pair), lambda b, j: (b, 0, j))
    const_spec = pl.BlockSpec((2, pair, seq), lambda b, j: (0, 0, 0))
    out = pl.pallas_call(
        functools.partial(_moba_prompt_kernel, n_blocks=n_blocks),
        grid=(bsz, W_ATT // pair),
        in_specs=[pl.BlockSpec(memory_space=pltpu.SMEM), qo_spec, kv_spec, kv_spec, const_spec, const_spec],
        out_specs=qo_spec,
        out_shape=jax.ShapeDtypeStruct((bsz, seq, W_ATT), F32),
        scratch_shapes=[pltpu.VMEM((2, pair, seq), BF16), pltpu.VMEM((2, pair, seq), BF16)],
        compiler_params=_cparams(("parallel", "parallel")),
    )(slopes, q.reshape(bsz, seq, W_ATT), kt, vt, kc, vc)
    return out.reshape(bsz * seq, W_ATT)


PAGES_PER_STEP = 16


def _moba_sample_kernel(pt_ref, slope_ref, tq_ref, q_ref, kn_ref, vn_ref, *rest, n_pages, page, t_new):
    del pt_ref
    k_refs = rest[:PAGES_PER_STEP]
    v_refs = rest[PAGES_PER_STEP:2 * PAGES_PER_STEP]
    o_ref, st_sc, selb_sc, ksum_sc, qbd_sc, qbdf_sc, own_sc, inv_sc, acc_sc = rest[2 * PAGES_PER_STEP:]
    ph = pl.program_id(1)
    pg = pl.program_id(2)
    n_groups = n_pages // PAGES_PER_STEP
    ppb = MOBA_BLOCK // page
    n_blocks = n_pages // ppb
    past = n_pages * page
    ht = N_HEADS * t_new

    @pl.when((ph == 0) & (pg == 0))
    def _():
        qt = jnp.concatenate([q_ref[...]] * N_HEADS, axis=0)
        row = lax.broadcasted_iota(jnp.int32, (ht, W_ATT), 0)
        col = lax.broadcasted_iota(jnp.int32, (ht, W_ATT), 1)
        qbd = jnp.where(col // HEAD_DIM == row // t_new, qt, 0.0) * ATT_SCALE
        qbdf_sc[...] = qbd
        qbd_sc[...] = qbd.astype(BF16)
        ksum_sc[...] = jnp.zeros_like(ksum_sc)

    @pl.when(ph == 0)
    def _():
        lane = lax.broadcasted_iota(jnp.int32, (W_ATT, LANES), 1)
        for bb in range(PAGES_PER_STEP // ppb):
            tot = None
            for pp in range(ppb):
                ii = bb * ppb + pp
                kp = k_refs[ii][...]
                tot = kp if tot is None else tot + kp
                st_sc[pg * PAGES_PER_STEP + ii] = jnp.dot(qbd_sc[...], kp.astype(BF16), preferred_element_type=F32)
            blk_idx = pg * (PAGES_PER_STEP // ppb) + bb
            ksum_sc[...] = jnp.where(lane == blk_idx, jnp.sum(tot, axis=1, keepdims=True), ksum_sc[...])

    @pl.when((ph == 0) & (pg == n_groups - 1))
    def _():
        slope = slope_ref[...]
        tq = tq_ref[...]
        lane = lax.broadcasted_iota(jnp.int32, (ht, LANES), 1)
        lane_f = lane.astype(F32)
        gate = jnp.dot(qbdf_sc[...], ksum_sc[...], precision=HIGHEST, preferred_element_type=F32)
        gate = jnp.where(lane < n_blocks, gate, -jnp.inf)
        keep = _top_k_mask(gate, 0, n_blocks, min(MOBA_TOP_K, n_blocks)) & (lane < n_blocks)
        sel_bias = jnp.where(keep, 0.0, NEG)
        for n in range(n_blocks):
            selb_sc[n] = jnp.broadcast_to(sel_bias[:, n:n + 1], (ht, LANES))

        k_own = jnp.concatenate([kn_ref[...], jnp.zeros((LANES - t_new, W_ATT), F32)], axis=0)
        so = lax.dot_general(qbd_sc[...], k_own.astype(BF16), NT_DIMS, preferred_element_type=F32)
        lo = jnp.where((lane_f <= tq) & (lane < t_new), so - slope * (tq - lane_f), NEG)
        qpos = tq + float(past)

        def logits(n, pp):
            pidx = n * ppb + pp
            dist = qpos - (lane_f + (pidx * page).astype(F32))
            return pidx, st_sc[pidx] - slope * dist + selb_sc[n]

        def max_body(n, m_vec):
            for pp in range(ppb):
                m_vec = jnp.maximum(m_vec, logits(n, pp)[1])
            return m_vec

        m = jnp.max(lax.fori_loop(0, n_blocks, max_body, lo, unroll=4), axis=1, keepdims=True)
        eo = jnp.exp(lo - m)

        def exp_body(n, l_vec):
            for pp in range(ppb):
                pidx, lg = logits(n, pp)
                e = jnp.exp(lg - m)
                st_sc[pidx] = e
                l_vec = l_vec + e
            return l_vec

        l_vec = lax.fori_loop(0, n_blocks, exp_body, eo, unroll=4)
        inv = 1.0 / jnp.sum(l_vec, axis=1, keepdims=True)
        inv_sc[...] = jnp.broadcast_to(inv, inv_sc.shape)
        own_sc[...] = eo * inv
        acc_sc[...] = jnp.zeros_like(acc_sc)

    @pl.when(ph == 1)
    def _():
        inv = inv_sc[...]
        tot = None
        for ii in range(PAGES_PER_STEP):
            p = (st_sc[pg * PAGES_PER_STEP + ii] * inv).astype(BF16)
            d = lax.dot_general(p, v_refs[ii][...].astype(BF16), NT_DIMS, preferred_element_type=F32)
            tot = d if tot is None else tot + d
        acc_sc[...] += tot

    @pl.when((ph == 1) & (pg == n_groups - 1))
    def _():
        v_own = jnp.concatenate([vn_ref[...], jnp.zeros((LANES - t_new, W_ATT), F32)], axis=0)
        acc = acc_sc[...] + jnp.dot(own_sc[...].astype(BF16), v_own.astype(BF16), preferred_element_type=F32)
        col = lax.broadcasted_iota(jnp.int32, (t_new, W_ATT), 1)
        out = jnp.zeros((t_new, W_ATT), F32)
        for h in range(N_HEADS):
            out += jnp.where(col // HEAD_DIM == h, acc[h * t_new:(h + 1) * t_new, :], 0.0)
        o_ref[...] = out


def _moba_sample(q, k, v, cache_kt, cache_vt, layer, page_table, slopes_np, bsz, t_new):
    page = cache_kt.shape[-1]
    n_pages = page_table.shape[1]
    assert MOBA_BLOCK % page == 0 and (n_pages * page) % MOBA_BLOCK == 0 and n_pages % PAGES_PER_STEP == 0
    assert t_new % SUBLANES == 0 and t_new <= LANES and page == LANES
    ht = N_HEADS * t_new
    n_groups = n_pages // PAGES_PER_STEP
    n_blocks = n_pages * page // MOBA_BLOCK
    assert n_blocks <= LANES
    row_h = np.arange(ht) // t_new
    slope_r = jnp.asarray(np.broadcast_to(slopes_np[row_h][:, None], (ht, LANES)), F32)
    tq_r = jnp.asarray(np.broadcast_to((np.arange(ht) % t_new)[:, None], (ht, LANES)), F32)
    wide = lambda a: a.reshape(t_new, bsz * W_ATT)
    tok_spec = pl.BlockSpec((t_new, W_ATT), lambda b, ph, pg, pt: (0, b))
    const_spec = pl.BlockSpec((ht, LANES), lambda b, ph, pg, pt: (0, 0))

    def k_map(ii):
        def f(b, ph, pg, pt):
            p = jnp.where(ph == 0, pg, n_groups - 1) * PAGES_PER_STEP + ii
            return (layer, pt[b * n_pages + p], 0, 0)
        return f

    def v_map(ii):
        def f(b, ph, pg, pt):
            p = jnp.where(ph == 1, pg, 0) * PAGES_PER_STEP + ii
            return (layer, pt[b * n_pages + p], 0, 0)
        return f

    page_block = (None, None, W_ATT, page)
    grid_spec = pltpu.PrefetchScalarGridSpec(
        num_scalar_prefetch=1,
        grid=(bsz, 2, n_groups),
        in_specs=[const_spec, const_spec, tok_spec, tok_spec, tok_spec]
                 + [pl.BlockSpec(page_block, k_map(ii)) for ii in range(PAGES_PER_STEP)]
                 + [pl.BlockSpec(page_block, v_map(ii)) for ii in range(PAGES_PER_STEP)],
        out_specs=tok_spec,
        scratch_shapes=[pltpu.VMEM((n_pages, ht, page), F32),
                        pltpu.VMEM((n_blocks, ht, LANES), F32),
                        pltpu.VMEM((W_ATT, LANES), F32),
                        pltpu.VMEM((ht, W_ATT), BF16),
                        pltpu.VMEM((ht, W_ATT), F32),
                        pltpu.VMEM((ht, LANES), F32),
                        pltpu.VMEM((ht, LANES), F32),
                        pltpu.VMEM((ht, W_ATT), F32)],
    )
    out = pl.pallas_call(
        functools.partial(_moba_sample_kernel, n_pages=n_pages, page=page, t_new=t_new),
        grid_spec=grid_spec,
        out_shape=jax.ShapeDtypeStruct((t_new, bsz * W_ATT), F32),
        compiler_params=_cparams(("parallel", "arbitrary", "arbitrary")),
    )(page_table.reshape(-1), slope_r, tq_r, wide(q), wide(k), wide(v), *([cache_kt] * PAGES_PER_STEP),
      *([cache_vt] * PAGES_PER_STEP))
    return out.reshape(t_new * bsz, W_ATT)


SCAN_LANES = 256


def _gelu_tanh(x):
    return 0.5 * x * (1.0 + jnp.tanh(np.float32(np.sqrt(2.0 / np.pi)) * (x + 0.044715 * (x * x * x))))


def _ssm_kernel(u_ref, h0r_ref, h0i_ref, a_ref, bbd_ref, cbd_ref, d_ref, wglu_ref,
                y_ref, hr_ref, hi_ref, xs_sc, hr_sc, hi_sc, *, bsz, t_chunk):
    step = pl.program_id(0)

    @pl.when(step == 0)
    def _():
        hr_sc[...] = h0r_ref[...]
        hi_sc[...] = h0i_ref[...]

    u = u_ref[...]
    xs_sc[...] = jnp.dot(u.astype(BF16), bbd_ref[...], preferred_element_type=F32)
    for lc in range(N_STATE // SCAN_LANES):
        re = slice(lc * SCAN_LANES, (lc + 1) * SCAN_LANES)
        im = slice(N_STATE + lc * SCAN_LANES, N_STATE + (lc + 1) * SCAN_LANES)
        ar = jnp.broadcast_to(a_ref[0:1, re], (bsz, SCAN_LANES))
        ai = jnp.broadcast_to(a_ref[1:2, re], (bsz, SCAN_LANES))

        def scan_step(t, carry, re=re, im=im, ar=ar, ai=ai):
            hr, hi = carry
            rows = pl.ds(pl.multiple_of(t * bsz, bsz), bsz)
            nr = ar * hr - ai * hi + xs_sc[rows, re]
            ni = ar * hi + ai * hr + xs_sc[rows, im]
            xs_sc[rows, re] = nr
            xs_sc[rows, im] = ni
            return nr, ni

        hr, hi = lax.fori_loop(0, t_chunk, scan_step, (hr_sc[:, re], hi_sc[:, re]))
        hr_sc[:, re] = hr
        hi_sc[:, re] = hi

    y = jnp.dot(xs_sc[...].astype(BF16), cbd_ref[...], preferred_element_type=F32) + d_ref[...] * u
    y = _gelu_tanh(y)
    y_ref[...] = y * jax.nn.sigmoid(jnp.dot(y.astype(BF16), wglu_ref[...], preferred_element_type=F32))

    @pl.when(step == pl.num_programs(0) - 1)
    def _():
        hr_ref[...] = hr_sc[...]
        hi_ref[...] = hi_sc[...]


def _ssm(u_tb, h0r, h0i, a, bbd, cbd, d, wglu, bsz, seq, t_chunk):
    assert bsz % SUBLANES == 0 and seq % t_chunk == 0
    rows = t_chunk * bsz
    blk = pl.BlockSpec((rows, W_SSM), lambda s: (s, 0))
    st_spec = _full((bsz, N_STATE))
    return pl.pallas_call(
        functools.partial(_ssm_kernel, bsz=bsz, t_chunk=t_chunk),
        grid=(seq // t_chunk,),
        in_specs=[blk, st_spec, st_spec, _full(a.shape), _full(bbd.shape), _full(cbd.shape), _full(d.shape),
                  _full(wglu.shape)],
        out_specs=[blk, st_spec, st_spec],
        out_shape=[jax.ShapeDtypeStruct((seq * bsz, W_SSM), F32), jax.ShapeDtypeStruct((bsz, N_STATE), F32),
                   jax.ShapeDtypeStruct((bsz, N_STATE), F32)],
        scratch_shapes=[pltpu.VMEM((rows, 2 * N_STATE), F32), pltpu.VMEM((bsz, N_STATE), F32),
                        pltpu.VMEM((bsz, N_STATE), F32)],
        compiler_params=_cparams(("arbitrary",)),
    )(u_tb, h0r, h0i, a, bbd, cbd, d, wglu)


def _mix_out_kernel(*refs, tm, halo, row_stride, tiles_per_seq):
    if halo:
        (att_ref, cb_ref, cc_ref, cx_ref, cch_ref, cxh_ref, zst_ref, y_ref, x_ref, gb_ref, wout_ref, cw_ref,
         cbias_ref, xo_ref, zlast_ref, zs_sc) = refs
    else:
        (att_ref, cb_ref, cc_ref, cx_ref, zst_ref, y_ref, x_ref, gb_ref, wout_ref, cw_ref,
         cbias_ref, xo_ref, zlast_ref, zs_sc) = refs
    pre = zs_sc.shape[0] - tm
    z = cc_ref[...] * cx_ref[...]
    if halo:
        first = pl.program_id(0) % tiles_per_seq == 0
        zs_sc[0:pre, :] = jnp.where(first, zst_ref[...], cch_ref[...] * cxh_ref[...])
    else:
        zs_sc[0:pre, :] = zst_ref[...]
    zs_sc[pre:, :] = z
    yc = cbias_ref[...]
    for tap in range(CONV_WIDTH - 1):
        off = pre - (CONV_WIDTH - 1 - tap) * row_stride
        yc = yc + cw_ref[tap:tap + 1, :] * zs_sc[off:off + tm, :]
    yc = yc + cw_ref[CONV_WIDTH - 1:CONV_WIDTH, :] * z
    conv_out = cb_ref[...] * yc
    zlast_ref[...] = zs_sc[tm:tm + pre, :]

    acc = x_ref[...]
    col = 0
    for branch in (att_ref[...], conv_out, y_ref[...]):
        width = branch.shape[-1]
        nb = (_rms(branch) * gb_ref[:, col:col + width]).astype(BF16)
        acc = acc + jnp.dot(nb, wout_ref[col:col + width, :], preferred_element_type=F32)
        col += width
    xo_ref[...] = acc


def _mix_out(att, cb, cc, cx, zstate, y_arr, y_spec, x, gb, wout, cw, cbias, tm, halo, row_stride, tiles_per_seq,
             zstate_spec):
    n = x.shape[0]
    pre = zstate_spec.block_shape[-2]
    rows = lambda w: pl.BlockSpec((tm, w), lambda i: (i, 0))
    in_specs = [rows(W_ATT), rows(W_CONV), rows(W_CONV), rows(W_CONV)]
    args = [att, cb, cc, cx]
    if halo:
        per = tm // SUBLANES
        halo_spec = pl.BlockSpec((SUBLANES, W_CONV), lambda i: (jnp.maximum(i * per - 1, 0), 0))
        in_specs += [halo_spec, halo_spec]
        args += [cc, cx]
    in_specs += [zstate_spec, y_spec, rows(D_MODEL), _full(gb.shape), _full(wout.shape), _full(cw.shape),
                 _full(cbias.shape)]
    args += [zstate, y_arr, x, gb, wout, cw, cbias]
    n_tiles = n // tm
    return pl.pallas_call(
        functools.partial(_mix_out_kernel, tm=tm, halo=halo, row_stride=row_stride, tiles_per_seq=tiles_per_seq),
        grid=(n_tiles,),
        in_specs=in_specs,
        out_specs=[rows(D_MODEL), pl.BlockSpec((pre, W_CONV), lambda i: (i, 0))],
        out_shape=[jax.ShapeDtypeStruct((n, D_MODEL), F32), jax.ShapeDtypeStruct((n_tiles * pre, W_CONV), F32)],
        scratch_shapes=[pltpu.VMEM((tm + pre, W_CONV), F32)],
        compiler_params=_cparams(("parallel",)),
    )(*args)


def _ffn_kernel(*refs, moe, final):
    refs = list(refs)
    x_ref, g_ref = refs[:2]
    pos = 2
    if moe:
        wr_ref, br_ref = refs[pos:pos + 2]
        pos += 2
    wg_ref, wu_ref, wd_ref = refs[pos:pos + 3]
    pos += 3
    if final:
        gf_ref = refs[pos]
        pos += 1
    o_ref, h_sc, acc_sc = refs[pos:pos + 3]
    pos += 3
    if moe:
        comb_sc = refs[pos]
    e = pl.program_id(1)
    c = pl.program_id(2)
    last = (e == pl.num_programs(1) - 1) & (c == pl.num_programs(2) - 1)

    @pl.when((e == 0) & (c == 0))
    def _():
        h = _rms(x_ref[...]) * g_ref[...]
        h_sc[...] = h.astype(BF16)
        acc_sc[...] = jnp.zeros_like(acc_sc)
        if moe:
            logits = jnp.dot(h, wr_ref[...], precision=HIGHEST, preferred_element_type=F32) + br_ref[...]
            lane = lax.broadcasted_iota(jnp.int32, logits.shape, 1)
            logits = jnp.where(lane < N_EXPERTS, logits, -jnp.inf)
            keep = _top_k_mask(logits, 0, N_EXPERTS, EXPERT_TOP_K) & (lane < N_EXPERTS)
            top = jnp.max(logits, axis=1, keepdims=True)
            w = jnp.where(keep, jnp.exp(logits - top), 0.0)
            comb_sc[...] = w / jnp.sum(w, axis=1, keepdims=True)

    hb = h_sc[...]
    a = jnp.dot(hb, wg_ref[...], preferred_element_type=F32)
    b = jnp.dot(hb, wu_ref[...], preferred_element_type=F32)
    t = (a * jax.nn.sigmoid(a) * b).astype(BF16)
    d = jnp.dot(t, wd_ref[...], preferred_element_type=F32)
    if moe:
        d = d * _lane_column(comb_sc[...], e)
    acc_sc[...] += d

    @pl.when(last)
    def _():
        out = x_ref[...] + acc_sc[...]
        if final:
            out = _rms(out) * gf_ref[...]
        o_ref[...] = out


def _ffn(x, g, wg, wu, wd, tm, f_chunk, router=None, g_final=None):
    n = x.shape[0]
    n_exp, _, f_dim = wg.shape
    moe = router is not None
    final = g_final is not None
    rows = pl.BlockSpec((tm, D_MODEL), lambda i, e, c: (i, 0))
    const = lambda shape: pl.BlockSpec(shape, lambda i, e, c: (0,) * len(shape))
    in_specs = [rows, const((1, D_MODEL))]
    args = [x, g]
    if moe:
        in_specs += [const(router[0].shape), const(router[1].shape)]
        args += list(router)
    in_specs += [pl.BlockSpec((None, D_MODEL, f_chunk), lambda i, e, c: (e, 0, c)),
                 pl.BlockSpec((None, D_MODEL, f_chunk), lambda i, e, c: (e, 0, c)),
                 pl.BlockSpec((None, f_chunk, D_MODEL), lambda i, e, c: (e, c, 0))]
    args += [wg, wu, wd]
    if final:
        in_specs.append(const((1, D_MODEL)))
        args.append(g_final)
    scratch = [pltpu.VMEM((tm, D_MODEL), BF16), pltpu.VMEM((tm, D_MODEL), F32)]
    if moe:
        scratch.append(pltpu.VMEM((tm, LANES), F32))
    return pl.pallas_call(
        functools.partial(_ffn_kernel, moe=moe, final=final),
        grid=(n // tm, n_exp, f_dim // f_chunk),
        in_specs=in_specs,
        out_specs=rows,
        out_shape=jax.ShapeDtypeStruct((n, D_MODEL), F32),
        scratch_shapes=scratch,
        compiler_params=_cparams(("parallel", "arbitrary", "arbitrary")),
    )(*args)


def _alibi_slopes_np():
    return (2.0 ** (-8.0 * np.arange(1, N_HEADS + 1) / N_HEADS)).astype(np.float32)


def _ssm_params(a_re, a_im, log_dt, b_re, b_im, c_re, c_im):
    ar, ai = a_re.astype(F32), a_im.astype(F32)
    dt = jnp.exp(log_dt.astype(F32))[:, None]
    mag = jnp.exp(dt * ar)
    abr, abi = mag * jnp.cos(dt * ai), mag * jnp.sin(dt * ai)
    den = ar * ar + ai * ai
    zr = ((abr - 1.0) * ar + abi * ai) / den
    zi = (abi * ar - (abr - 1.0) * ai) / den
    br, bim = b_re.astype(F32), b_im.astype(F32)
    bbr = zr[..., None] * br - zi[..., None] * bim
    bbi = zr[..., None] * bim + zi[..., None] * br
    eye = jnp.eye(N_GROUPS, dtype=F32)
    to_in = lambda m: jnp.einsum('gnc,gh->gchn', m, eye).reshape(W_SSM, N_STATE)
    to_out = lambda m: jnp.einsum('gcn,gh->gnhc', m, eye).reshape(N_STATE, W_SSM)
    bbd = jnp.concatenate([to_in(bbr), to_in(bbi)], axis=1).astype(BF16)
    cbd = jnp.concatenate([to_out(c_re.astype(F32)), -to_out(c_im.astype(F32))], axis=0).astype(BF16)
    a = jnp.stack([abr.reshape(N_STATE), abi.reshape(N_STATE)])
    return a, bbd, cbd


def _trunk(x, w, depth, bsz, seq, time_major, attend, conv0, h0r, h0i):
    n = bsz * seq
    ks, vs, zs, hrs, his = [], [], [], [], []
    if time_major:
        tm, tiles_per_seq, row_stride, halo = n, 1, bsz, False
        pre = (CONV_WIDTH - 1) * bsz
        u_shape = (n, W_SSM)
        u_spec = pl.BlockSpec((tm, W_SSM), lambda i: (0, 0))
        kv_shape = (n, W_ATT)
        kv_spec = pl.BlockSpec((tm, W_ATT), lambda i: (0, 0))
        zstate_spec = pl.BlockSpec((pre, W_CONV), lambda i: (0, 0))
        t_chunk = seq
    else:
        tm, row_stride, halo = ROW_TILE, 1, True
        tiles_per_seq = seq // tm
        pre = SUBLANES
        u_shape = (seq, bsz * W_SSM)
        u_spec = pl.BlockSpec((tm, W_SSM), lambda i: (i % tiles_per_seq, i // tiles_per_seq))
        kv_shape = (bsz, W_ATT, seq)
        kv_spec = pl.BlockSpec((None, W_ATT, tm), lambda i: (i // tiles_per_seq, 0, i % tiles_per_seq))
        zstate_spec = pl.BlockSpec((None, pre, W_CONV), lambda i: (i // tiles_per_seq, 0, 0))
        t_chunk = 32
    for l in range(depth):
        q, k, v, cb, cc, cx, u = _mix_in(x, w['g_mix'][l], w['w_in'][l], w['w_kvt'][l], tm, u_shape, u_spec,
                                         kv_shape, kv_spec)
        att = attend(l, q, k, v)
        y, hr, hi = _ssm(u.reshape(seq * bsz, W_SSM), h0r[l], h0i[l], *w['ssm'][l], bsz, seq, t_chunk)
        x, zlast = _mix_out(att, cb, cc, cx, conv0[l], y.reshape(u_shape), u_spec, x, w['g_branch'][l], w['w_out'][l],
                            w['conv_w'][l], w['conv_b'][l], tm, halo, row_stride, tiles_per_seq, zstate_spec)
        g_final = w['g_final'] if l == depth - 1 else None
        if l % 2 == 0:
            x = _ffn(x, w['g_ffn'][l], *w['dense'][l // 2], tm, w['dense'][l // 2][0].shape[-1] // 2, g_final=g_final)
        else:
            wg, wu, wd, router = w['moe'][l // 2]
            x = _ffn(x, w['g_ffn'][l], wg, wu, wd, tm, wg.shape[-1], router=router, g_final=g_final)
        ks.append(k)
        vs.append(v)
        zs.append(zlast)
        hrs.append(hr)
        his.append(hi)
    return x, ks, vs, zs, hrs, his


def kernel(x_prompt, x_sample, cache_k, cache_v, state_conv, state_ssm_re, state_ssm_im, page_table, g_mix_norm, w_in, g_branch, w_out, conv_w, conv_b, ssm_a_re, ssm_a_im, ssm_log_dt, ssm_b_re, ssm_b_im, ssm_c_re, ssm_c_im, ssm_d, ssm_w_glu, g_ffn_norm, w_ffn_gate, w_ffn_up, w_ffn_down, w_router, b_router, w_exp_gate, w_exp_up, w_exp_down, g_final):
    depth = w_in.shape[0]
    bp, sp, _ = x_prompt.shape
    bs, ss, _ = x_sample.shape
    n_pool, page = cache_k.shape[1], cache_k.shape[2]
    slopes_np = _alibi_slopes_np()
    slopes = jnp.asarray(slopes_np)

    row = lambda a: a.reshape(a.shape[0], 1, a.shape[-1])
    w_in_bf16 = w_in.astype(BF16)
    w = {
        'g_mix': row(g_mix_norm), 'w_in': w_in_bf16, 'g_branch': row(g_branch), 'w_out': w_out.astype(BF16),
        'w_kvt': jnp.swapaxes(w_in_bf16[:, :, W_ATT:3 * W_ATT], 1, 2),
        'conv_w': conv_w, 'conv_b': row(conv_b), 'g_ffn': row(g_ffn_norm), 'g_final': g_final.reshape(1, D_MODEL),
        'ssm': [(*_ssm_params(ssm_a_re[l], ssm_a_im[l], ssm_log_dt[l], ssm_b_re[l], ssm_b_im[l], ssm_c_re[l],
                              ssm_c_im[l]), ssm_d[l].reshape(1, W_SSM), ssm_w_glu[l].astype(BF16))
                for l in range(depth)],
        'dense': [(w_ffn_gate[j:j + 1].astype(BF16), w_ffn_up[j:j + 1].astype(BF16), w_ffn_down[j:j + 1].astype(BF16))
                  for j in range(w_ffn_gate.shape[0])],
        'moe': [(w_exp_gate[j].astype(BF16), w_exp_up[j].astype(BF16), w_exp_down[j].astype(BF16),
                 (jnp.pad(w_router[j], ((0, 0), (0, LANES - N_EXPERTS))),
                  jnp.pad(b_router[j].reshape(1, N_EXPERTS), ((0, 0), (0, LANES - N_EXPERTS)))))
                for j in range(w_exp_gate.shape[0])],
    }

    conv0_p = jnp.zeros((depth, bp, SUBLANES, W_CONV), F32)
    h0_p = jnp.zeros((depth, bp, N_STATE), F32)
    attend_p = lambda l, q, kt, vt: _moba_prompt(q, kt, vt, slopes, bp, sp)
    yp, kts, vts, zs, hrs, his = _trunk(x_prompt.reshape(bp * sp, D_MODEL), w, depth, bp, sp, False, attend_p,
                                        conv0_p, h0_p, h0_p)
    y_prompt = yp.reshape(bp, sp, D_MODEL)
    from_t = lambda a: jnp.transpose(jnp.stack(a).reshape(depth, bp, N_HEADS, HEAD_DIM, sp), (0, 1, 4, 2, 3))
    k_p, v_p = from_t(kts), from_t(vts)
    tiles = sp // ROW_TILE
    conv_p = jnp.stack(zs).reshape(depth, bp, tiles, SUBLANES, W_CONV)[:, :, -1, SUBLANES - (CONV_WIDTH - 1):, :]
    sre_p = jnp.stack(hrs).reshape(depth, bp, N_GROUPS, SSM_STATE)
    sim_p = jnp.stack(his).reshape(depth, bp, N_GROUPS, SSM_STATE)

    to_t = lambda c: jnp.transpose(c, (0, 1, 3, 4, 2)).reshape(depth, n_pool, W_ATT, page)
    cache_kt, cache_vt = to_t(cache_k), to_t(cache_v)
    xs = jnp.swapaxes(x_sample, 0, 1).reshape(ss * bs, D_MODEL)
    conv0_s = jnp.swapaxes(state_conv, 1, 2).reshape(depth, (CONV_WIDTH - 1) * bs, W_CONV)
    h0r_s = state_ssm_re.reshape(depth, bs, N_STATE)
    h0i_s = state_ssm_im.reshape(depth, bs, N_STATE)
    attend_s = lambda l, q, k, v: _moba_sample(q, k, v, cache_kt, cache_vt, l, page_table, slopes_np, bs, ss)
    ys, ks, vs, zs, hrs, his = _trunk(xs, w, depth, bs, ss, True, attend_s, conv0_s, h0r_s, h0i_s)
    from_tb = lambda a, width: jnp.swapaxes(a.reshape(-1, ss, bs, width), 1, 2)
    y_sample = from_tb(ys, D_MODEL)[0]
    k_s = from_tb(jnp.stack(ks), W_ATT).reshape(depth, bs, ss, N_HEADS, HEAD_DIM)
    v_s = from_tb(jnp.stack(vs), W_ATT).reshape(depth, bs, ss, N_HEADS, HEAD_DIM)
    conv_s = jnp.swapaxes(jnp.stack(zs).reshape(depth, CONV_WIDTH - 1, bs, W_CONV), 1, 2)
    sre_s = jnp.stack(hrs).reshape(depth, bs, N_GROUPS, SSM_STATE)
    sim_s = jnp.stack(his).reshape(depth, bs, N_GROUPS, SSM_STATE)
    return (y_prompt, y_sample, k_p, v_p, conv_p, sre_p, sim_p, k_s, v_s, conv_s, sre_s, sim_s)
```

```python
import functools

import numpy as np
import jax
import jax.numpy as jnp
from jax import lax
from jax.experimental import pallas as pl
from jax.experimental.pallas import tpu as pltpu

F32 = jnp.float32
BF16 = jnp.bfloat16
HIGHEST = lax.Precision.HIGHEST

D_MODEL = 1024
N_HEADS = 8
HEAD_DIM = 64
W_ATT = N_HEADS * HEAD_DIM
W_CONV = 256
W_SSM = 256
CONV_WIDTH = 3
SSM_GROUP = 16
N_GROUPS = 16
SSM_STATE = 64
N_STATE = N_GROUPS * SSM_STATE
MOBA_BLOCK = 256
MOBA_TOP_K = 3
ATT_SCALE = HEAD_DIM ** -0.5
N_EXPERTS = 8
EXPERT_TOP_K = 2
RMS_EPS = 1e-6
NEG = -1e30

LANES = 128
SUBLANES = 8
VMEM_LIMIT = 56 * 1024 * 1024
ROW_TILE = 512
NT_DIMS = (((1,), (1,)), ((), ()))


def _cparams(sem):
    return pltpu.CompilerParams(dimension_semantics=sem, vmem_limit_bytes=VMEM_LIMIT)


def _rms(x):
    return x * lax.rsqrt(jnp.mean(x * x, axis=-1, keepdims=True) + RMS_EPS)


def _full(shape):
    n = len(shape)
    return pl.BlockSpec(shape, lambda *_: (0,) * n)


def _lane_column(x, idx):
    lane = lax.broadcasted_iota(jnp.int32, x.shape, 1)
    return jnp.sum(jnp.where(lane == idx, x, 0.0), axis=1, keepdims=True)


def _top_k_mask(score, first, n_cand, k):
    lane = lax.broadcasted_iota(jnp.int32, score.shape, 1)
    rank = jnp.zeros(score.shape, jnp.int32)
    for m in range(first, first + n_cand):
        sm = score[:, m:m + 1]
        rank += ((sm > score) | ((sm == score) & (m < lane))).astype(jnp.int32)
    return rank < k


def _mix_in_kernel(x_ref, g_ref, w_ref, wkvt_ref, *rest, kv_transposed):
    q_ref, k_ref, v_ref, cb_ref, cc_ref, cx_ref, u_ref = rest[-7:]
    h = (_rms(x_ref[...]) * g_ref[...]).astype(BF16)
    outs = (q_ref, k_ref, v_ref, cb_ref, cc_ref, cx_ref, u_ref)
    widths = (W_ATT,) * 3 + (W_CONV,) * 3 + (W_SSM,)
    col = 0
    for idx, (ref, width) in enumerate(zip(outs, widths)):
        if kv_transposed and idx in (1, 2):
            ref[...] = lax.dot_general(wkvt_ref[(idx - 1) * W_ATT:idx * W_ATT, :], h, NT_DIMS,
                                       preferred_element_type=F32)
        else:
            ref[...] = jnp.dot(h, w_ref[:, col:col + width], preferred_element_type=F32)
        col += width


def _mix_in(x, g, w_bf16, wkvt_bf16, tm, u_shape, u_spec, kv_shape, kv_spec, kv_prev=None):
    n = x.shape[0]
    rows = lambda w: pl.BlockSpec((tm, w), lambda i: (i, 0))
    conv_widths = (W_CONV, W_CONV, W_CONV)
    in_specs = [rows(D_MODEL), _full((1, D_MODEL)), _full(w_bf16.shape), _full(wkvt_bf16.shape)]
    args = [x, g, w_bf16, wkvt_bf16]
    aliases = {}
    if kv_prev is not None:
        in_specs += [pl.BlockSpec(memory_space=pl.ANY)] * 2
        args += list(kv_prev)
        aliases = {4: 1, 5: 2}
    return pl.pallas_call(
        functools.partial(_mix_in_kernel, kv_transposed=len(kv_shape) == 4),
        grid=(n // tm,),
        in_specs=in_specs,
        out_specs=[rows(W_ATT), kv_spec, kv_spec] + [rows(w) for w in conv_widths] + [u_spec],
        out_shape=[jax.ShapeDtypeStruct((n, W_ATT), F32), jax.ShapeDtypeStruct(kv_shape, F32),
                   jax.ShapeDtypeStruct(kv_shape, F32)]
                  + [jax.ShapeDtypeStruct((n, w), F32) for w in conv_widths] + [jax.ShapeDtypeStruct(u_shape, F32)],
        input_output_aliases=aliases,
        compiler_params=_cparams(("parallel",)),
    )(*args)


AUG_KPOS = 32


def _moba_prompt_consts(seq):
    n_blocks = seq // MOBA_BLOCK
    kc = np.zeros((2, 2 * HEAD_DIM, seq), np.float32)
    vc = np.zeros((2, 2 * HEAD_DIM, seq), np.float32)
    for hl in range(2):
        aug = HEAD_DIM * (1 - hl)
        for n in range(n_blocks):
            kc[hl, aug + n, n * MOBA_BLOCK:(n + 1) * MOBA_BLOCK] = 1.0
        kc[hl, aug + AUG_KPOS, :] = np.arange(seq) % MOBA_BLOCK
        vc[hl, aug, :] = 1.0
    return jnp.asarray(kc), jnp.asarray(vc)


def _moba_prompt_kernel(slopes_ref, q_ref, kt_ref, vt_ref, kc_ref, vc_ref, o_ref, kb_sc, vb_sc, *, n_blocks):
    j = pl.program_id(1)
    blk = MOBA_BLOCK
    pair = 2 * HEAD_DIM
    kt = kt_ref[...]
    vt = vt_ref[...]
    row = lax.broadcasted_iota(jnp.int32, kt.shape, 0)
    for hl in range(2):
        own_rows = (row >= HEAD_DIM * hl) & (row < HEAD_DIM * (hl + 1))
        kb_sc[hl] = jnp.where(own_rows, kt, kc_ref[hl]).astype(BF16)
        vb_sc[hl] = jnp.where(own_rows, vt, vc_ref[hl]).astype(BF16)

    krow = lax.broadcasted_iota(jnp.int32, (pair, LANES), 0)
    klane = lax.broadcasted_iota(jnp.int32, (pair, LANES), 1)
    aug_of_row = jnp.where(krow < HEAD_DIM, HEAD_DIM, 0)
    kmean = jnp.zeros((pair, LANES), F32)
    for n in range(n_blocks):
        mean_n = jnp.mean(kt[:, n * blk:(n + 1) * blk], axis=1, keepdims=True)
        kmean = jnp.where(klane == aug_of_row + n, mean_n, kmean)

    lane = lax.broadcasted_iota(jnp.int32, (blk, pair), 1)
    r_io = lax.broadcasted_iota(jnp.int32, (blk, blk), 0)
    c_io = lax.broadcasted_iota(jnp.int32, (blk, blk), 1)
    causal = c_io <= r_io
    for i in range(n_blocks):
        q_i = q_ref[i * blk:(i + 1) * blk, :]
        past, width = i * blk, (i + 1) * blk
        outs = []
        for hl in range(2):
            aug = HEAD_DIM * (1 - hl)
            slope = slopes_ref[2 * j + hl]
            qh = jnp.where((lane >= HEAD_DIM * hl) & (lane < HEAD_DIM * (hl + 1)), q_i, 0.0)
            is_past = (lane >= aug) & (lane < aug + i)
            if i > MOBA_TOP_K:
                gate = jnp.dot(qh, kmean, precision=HIGHEST, preferred_element_type=F32)
                keep = _top_k_mask(jnp.where(is_past, gate, -jnp.inf), aug, i, MOBA_TOP_K) & is_past
            else:
                keep = is_past
            bias = jnp.where(keep | (lane == aug + i), 0.0, NEG) + slope * (blk * (lane - (aug + i)).astype(F32))
            q_aug = jnp.where((lane >= aug) & (lane <= aug + i), bias, qh * ATT_SCALE)
            q_aug = jnp.where(lane == aug + AUG_KPOS, slope, q_aug).astype(BF16)

            s = jnp.dot(q_aug, kb_sc[hl, :, 0:width], preferred_element_type=F32)
            s_own = jnp.where(causal, s[:, past:width], NEG)
            m = jnp.max(s_own, axis=1, keepdims=True)
            if i > 0:
                m = jnp.maximum(m, jnp.max(s[:, 0:past], axis=1, keepdims=True))
            o = lax.dot_general(jnp.exp(s_own - m).astype(BF16), vb_sc[hl, :, past:width], NT_DIMS,
                                preferred_element_type=F32)
            if i > 0:
                o = o + lax.dot_general(jnp.exp(s[:, 0:past] - m).astype(BF16), vb_sc[hl, :, 0:past], NT_DIMS,
                                        preferred_element_type=F32)
            outs.append(o * (1.0 / o[:, aug:aug + 1]))
        o_ref[i * blk:(i + 1) * blk, :] = jnp.where(lane < HEAD_DIM, outs[0], outs[1])


def _moba_prompt(q, kt, vt, layer, slopes, bsz, seq):
    assert seq % MOBA_BLOCK == 0
    n_blocks = seq // MOBA_BLOCK
    assert n_blocks <= AUG_KPOS < HEAD_DIM
    pair = 2 * HEAD_DIM
    kc, vc = _moba_prompt_consts(seq)
    kv_spec = pl.BlockSpec((None, None, pair, seq), lambda b, j: (layer, b, j, 0))
    qo_spec = pl.BlockSpec((None, seq, pair), lambda b, j: (b, 0, j))
    const_spec = pl.BlockSpec((2, pair, seq), lambda b, j: (0, 0, 0))
    out = pl.pallas_call(
        functools.partial(_moba_prompt_kernel, n_blocks=n_blocks),
        grid=(bsz, W_ATT // pair),
        in_specs=[pl.BlockSpec(memory_space=pltpu.SMEM), qo_spec, kv_spec, kv_spec, const_spec, const_spec],
        out_specs=qo_spec,
        out_shape=jax.ShapeDtypeStruct((bsz, seq, W_ATT), F32),
        scratch_shapes=[pltpu.VMEM((2, pair, seq), BF16), pltpu.VMEM((2, pair, seq), BF16)],
        compiler_params=_cparams(("parallel", "parallel")),
    )(slopes, q.reshape(bsz, seq, W_ATT), kt, vt, kc, vc)
    return out.reshape(bsz * seq, W_ATT)


PAGES_PER_STEP = 16


def _moba_sample_kernel(pt_ref, slope_ref, tq_ref, q_ref, kn_ref, vn_ref, *rest, n_pages, page, t_new):
    del pt_ref
    k_refs = rest[:PAGES_PER_STEP]
    v_refs = rest[PAGES_PER_STEP:2 * PAGES_PER_STEP]
    o_ref, st_sc, selb_sc, ksum_sc, qbd_sc, qbdf_sc, own_sc, inv_sc, acc_sc = rest[2 * PAGES_PER_STEP:]
    ph = pl.program_id(1)
    pg = pl.program_id(2)
    n_groups = n_pages // PAGES_PER_STEP
    ppb = MOBA_BLOCK // page
    n_blocks = n_pages // ppb
    past = n_pages * page
    ht = N_HEADS * t_new

    @pl.when((ph == 0) & (pg == 0))
    def _():
        qt = jnp.concatenate([q_ref[...]] * N_HEADS, axis=0)
        row = lax.broadcasted_iota(jnp.int32, (ht, W_ATT), 0)
        col = lax.broadcasted_iota(jnp.int32, (ht, W_ATT), 1)
        qbd = jnp.where(col // HEAD_DIM == row // t_new, qt, 0.0) * ATT_SCALE
        qbdf_sc[...] = qbd
        qbd_sc[...] = qbd.astype(BF16)
        ksum_sc[...] = jnp.zeros_like(ksum_sc)

    @pl.when(ph == 0)
    def _():
        lane = lax.broadcasted_iota(jnp.int32, (W_ATT, LANES), 1)
        for bb in range(PAGES_PER_STEP // ppb):
            tot = None
            for pp in range(ppb):
                ii = bb * ppb + pp
                kp = k_refs[ii][...]
                tot = kp if tot is None else tot + kp
                st_sc[pg * PAGES_PER_STEP + ii] = jnp.dot(qbd_sc[...], kp.astype(BF16), preferred_element_type=F32)
            blk_idx = pg * (PAGES_PER_STEP // ppb) + bb
            ksum_sc[...] = jnp.where(lane == blk_idx, jnp.sum(tot, axis=1, keepdims=True), ksum_sc[...])

    @pl.when((ph == 0) & (pg == n_groups - 1))
    def _():
        slope = slope_ref[...]
        tq = tq_ref[...]
        lane = lax.broadcasted_iota(jnp.int32, (ht, LANES), 1)
        lane_f = lane.astype(F32)
        gate = jnp.dot(qbdf_sc[...], ksum_sc[...], precision=HIGHEST, preferred_element_type=F32)
        gate = jnp.where(lane < n_blocks, gate, -jnp.inf)
        keep = _top_k_mask(gate, 0, n_blocks, min(MOBA_TOP_K, n_blocks)) & (lane < n_blocks)
        sel_bias = jnp.where(keep, 0.0, NEG)
        for n in range(n_blocks):
            selb_sc[n] = jnp.broadcast_to(sel_bias[:, n:n + 1], (ht, LANES))

        k_own = jnp.concatenate([kn_ref[...], jnp.zeros((LANES - t_new, W_ATT), F32)], axis=0)
        so = lax.dot_general(qbd_sc[...], k_own.astype(BF16), NT_DIMS, preferred_element_type=F32)
        lo = jnp.where((lane_f <= tq) & (lane < t_new), so - slope * (tq - lane_f), NEG)
        qpos = tq + float(past)

        def logits(n, pp):
            pidx = n * ppb + pp
            dist = qpos - (lane_f + jnp.asarray(pidx * page, F32))
            return pidx, st_sc[pidx] - slope * dist + selb_sc[n]

        def max_body(n, m_vec):
            for pp in range(ppb):
                m_vec = jnp.maximum(m_vec, logits(n, pp)[1])
            return m_vec

        m = jnp.max(lax.fori_loop(0, n_blocks, max_body, lo, unroll=4), axis=1, keepdims=True)
        eo = jnp.exp(lo - m)

        def exp_body(n, l_vec):
            for pp in range(ppb):
                pidx, lg = logits(n, pp)
                e = jnp.exp(lg - m)
                st_sc[pidx] = e
                l_vec = l_vec + e
            return l_vec

        l_vec = lax.fori_loop(0, n_blocks, exp_body, eo, unroll=4)
        inv = 1.0 / jnp.sum(l_vec, axis=1, keepdims=True)
        inv_sc[...] = jnp.broadcast_to(inv, inv_sc.shape)
        own_sc[...] = eo * inv
        acc_sc[...] = jnp.zeros_like(acc_sc)

    @pl.when(ph == 1)
    def _():
        inv = inv_sc[...]
        tot = None
        for ii in range(PAGES_PER_STEP):
            p = (st_sc[pg * PAGES_PER_STEP + ii] * inv).astype(BF16)
            d = lax.dot_general(p, v_refs[ii][...].astype(BF16), NT_DIMS, preferred_element_type=F32)
            tot = d if tot is None else tot + d
        acc_sc[...] += tot

    @pl.when((ph == 1) & (pg == n_groups - 1))
    def _():
        v_own = jnp.concatenate([vn_ref[...], jnp.zeros((LANES - t_new, W_ATT), F32)], axis=0)
        acc = acc_sc[...] + jnp.dot(own_sc[...].astype(BF16), v_own.astype(BF16), preferred_element_type=F32)
        col = lax.broadcasted_iota(jnp.int32, (t_new, W_ATT), 1)
        out = jnp.zeros((t_new, W_ATT), F32)
        for h in range(N_HEADS):
            out += jnp.where(col // HEAD_DIM == h, acc[h * t_new:(h + 1) * t_new, :], 0.0)
        o_ref[...] = out


def _moba_sample(q, k, v, cache_kt, cache_vt, layer, page_table, slopes_np, bsz, t_new):
    page = cache_kt.shape[-1]
    n_pages = page_table.shape[1]
    assert MOBA_BLOCK % page == 0 and (n_pages * page) % MOBA_BLOCK == 0 and n_pages % PAGES_PER_STEP == 0
    assert t_new % SUBLANES == 0 and t_new <= LANES and page == LANES
    ht = N_HEADS * t_new
    n_groups = n_pages // PAGES_PER_STEP
    n_blocks = n_pages * page // MOBA_BLOCK
    assert n_blocks <= LANES
    row_h = np.arange(ht) // t_new
    slope_r = jnp.asarray(np.broadcast_to(slopes_np[row_h][:, None], (ht, LANES)), F32)
    tq_r = jnp.asarray(np.broadcast_to((np.arange(ht) % t_new)[:, None], (ht, LANES)), F32)
    wide = lambda a: a.reshape(t_new, bsz * W_ATT)
    tok_spec = pl.BlockSpec((t_new, W_ATT), lambda b, ph, pg, pt: (0, b))
    const_spec = pl.BlockSpec((ht, LANES), lambda b, ph, pg, pt: (0, 0))

    def k_map(ii):
        def f(b, ph, pg, pt):
            p = jnp.where(ph == 0, pg, n_groups - 1) * PAGES_PER_STEP + ii
            return (layer, pt[b * n_pages + p], 0, 0)
        return f

    def v_map(ii):
        def f(b, ph, pg, pt):
            p = jnp.where(ph == 1, pg, 0) * PAGES_PER_STEP + ii
            return (layer, pt[b * n_pages + p], 0, 0)
        return f

    page_block = (None, None, W_ATT, page)
    grid_spec = pltpu.PrefetchScalarGridSpec(
        num_scalar_prefetch=1,
        grid=(bsz, 2, n_groups),
        in_specs=[const_spec, const_spec, tok_spec, tok_spec, tok_spec]
                 + [pl.BlockSpec(page_block, k_map(ii)) for ii in range(PAGES_PER_STEP)]
                 + [pl.BlockSpec(page_block, v_map(ii)) for ii in range(PAGES_PER_STEP)],
        out_specs=tok_spec,
        scratch_shapes=[pltpu.VMEM((n_pages, ht, page), F32),
                        pltpu.VMEM((n_blocks, ht, LANES), F32),
                        pltpu.VMEM((W_ATT, LANES), F32),
                        pltpu.VMEM((ht, W_ATT), BF16),
                        pltpu.VMEM((ht, W_ATT), F32),
                        pltpu.VMEM((ht, LANES), F32),
                        pltpu.VMEM((ht, LANES), F32),
                        pltpu.VMEM((ht, W_ATT), F32)],
    )
    out = pl.pallas_call(
        functools.partial(_moba_sample_kernel, n_pages=n_pages, page=page, t_new=t_new),
        grid_spec=grid_spec,
        out_shape=jax.ShapeDtypeStruct((t_new, bsz * W_ATT), F32),
        compiler_params=_cparams(("parallel", "arbitrary", "arbitrary")),
    )(page_table.reshape(-1), slope_r, tq_r, wide(q), wide(k), wide(v), *([cache_kt] * PAGES_PER_STEP),
      *([cache_vt] * PAGES_PER_STEP))
    return out.reshape(t_new * bsz, W_ATT)


SCAN_LANES = 256


def _gelu_tanh(x):
    return 0.5 * x * (1.0 + jnp.tanh(np.float32(np.sqrt(2.0 / np.pi)) * (x + 0.044715 * (x * x * x))))


def _ssm_kernel(u_ref, h0r_ref, h0i_ref, a_ref, bbd_ref, cbd_ref, d_ref, wglu_ref,
                y_ref, hr_ref, hi_ref, xs_sc, hr_sc, hi_sc, *wide_sc, bsz, t_chunk):
    step = pl.program_id(0)

    @pl.when(step == 0)
    def _():
        hr_sc[...] = h0r_ref[...]
        hi_sc[...] = h0i_ref[...]

    n_half = W_SSM // LANES
    if wide_sc:
        u_scs, y_scs = wide_sc[:n_half], wide_sc[n_half:]
        for b in range(bsz):
            for hf in range(n_half):
                col = b * W_SSM + hf * LANES
                u_scs[hf][pl.ds(b, t_chunk, stride=bsz), :] = u_ref[:, col:col + LANES]
        u = jnp.concatenate([sc[...] for sc in u_scs], axis=1)
    else:
        u = u_ref[...]
    xs_sc[...] = jnp.dot(u.astype(BF16), bbd_ref[...], preferred_element_type=F32)
    for lc in range(N_STATE // SCAN_LANES):
        re = slice(lc * SCAN_LANES, (lc + 1) * SCAN_LANES)
        im = slice(N_STATE + lc * SCAN_LANES, N_STATE + (lc + 1) * SCAN_LANES)
        ar = jnp.broadcast_to(a_ref[0:1, re], (bsz, SCAN_LANES))
        ai = jnp.broadcast_to(a_ref[1:2, re], (bsz, SCAN_LANES))

        def scan_step(t, carry, re=re, im=im, ar=ar, ai=ai):
            hr, hi = carry
            rows = pl.ds(pl.multiple_of(t * bsz, bsz), bsz)
            nr = ar * hr - ai * hi + xs_sc[rows, re]
            ni = ar * hi + ai * hr + xs_sc[rows, im]
            xs_sc[rows, re] = nr
            xs_sc[rows, im] = ni
            return nr, ni

        hr, hi = lax.fori_loop(0, t_chunk, scan_step, (hr_sc[:, re], hi_sc[:, re]))
        hr_sc[:, re] = hr
        hi_sc[:, re] = hi

    y = jnp.dot(xs_sc[...].astype(BF16), cbd_ref[...], preferred_element_type=F32) + d_ref[...] * u
    y = _gelu_tanh(y)
    y = y * jax.nn.sigmoid(jnp.dot(y.astype(BF16), wglu_ref[...], preferred_element_type=F32))
    if wide_sc:
        for hf in range(n_half):
            y_scs[hf][...] = y[:, hf * LANES:(hf + 1) * LANES]
        for b in range(bsz):
            for hf in range(n_half):
                col = b * W_SSM + hf * LANES
                y_ref[:, col:col + LANES] = y_scs[hf][pl.ds(b, t_chunk, stride=bsz), :]
    else:
        y_ref[...] = y

    @pl.when(step == pl.num_programs(0) - 1)
    def _():
        hr_ref[...] = hr_sc[...]
        hi_ref[...] = hi_sc[...]


def _ssm(u, h0r, h0i, a, bbd, cbd, d, wglu, bsz, seq, t_chunk):
    assert bsz % SUBLANES == 0 and seq % t_chunk == 0 and u.shape[0] in (seq, seq * bsz)
    rows = t_chunk * bsz
    wide = u.shape[0] == seq
    blk = pl.BlockSpec((t_chunk, bsz * W_SSM) if wide else (rows, W_SSM), lambda s: (s, 0))
    st_spec = _full((bsz, N_STATE))
    scratch = [pltpu.VMEM((rows, 2 * N_STATE), F32), pltpu.VMEM((bsz, N_STATE), F32), pltpu.VMEM((bsz, N_STATE), F32)]
    if wide:
        scratch += [pltpu.VMEM((rows, LANES), F32)] * (2 * (W_SSM // LANES))
    return pl.pallas_call(
        functools.partial(_ssm_kernel, bsz=bsz, t_chunk=t_chunk),
        grid=(seq // t_chunk,),
        in_specs=[blk, st_spec, st_spec, _full(a.shape), _full(bbd.shape), _full(cbd.shape), _full(d.shape),
                  _full(wglu.shape)],
        out_specs=[blk, st_spec, st_spec],
        out_shape=[jax.ShapeDtypeStruct(u.shape, F32), jax.ShapeDtypeStruct((bsz, N_STATE), F32),
                   jax.ShapeDtypeStruct((bsz, N_STATE), F32)],
        scratch_shapes=scratch,
        compiler_params=_cparams(("arbitrary",)),
    )(u, h0r, h0i, a, bbd, cbd, d, wglu)


def _mix_out_kernel(*refs, tm, halo, row_stride, tiles_per_seq):
    if halo:
        (att_ref, cb_ref, cc_ref, cx_ref, cch_ref, cxh_ref, zst_ref, y_ref, x_ref, gb_ref, wout_ref, cw_ref,
         cbias_ref, xo_ref, zlast_ref, zs_sc) = refs
    else:
        (att_ref, cb_ref, cc_ref, cx_ref, zst_ref, y_ref, x_ref, gb_ref, wout_ref, cw_ref,
         cbias_ref, xo_ref, zlast_ref, zs_sc) = refs
    pre = zs_sc.shape[0] - tm
    z = cc_ref[...] * cx_ref[...]
    if halo:
        first = pl.program_id(0) % tiles_per_seq == 0
        zs_sc[0:pre, :] = jnp.where(first, zst_ref[...], cch_ref[...] * cxh_ref[...])
    else:
        zs_sc[0:pre, :] = zst_ref[...]
    zs_sc[pre:, :] = z
    yc = cbias_ref[...]
    for tap in range(CONV_WIDTH - 1):
        off = pre - (CONV_WIDTH - 1 - tap) * row_stride
        yc = yc + cw_ref[tap:tap + 1, :] * zs_sc[off:off + tm, :]
    yc = yc + cw_ref[CONV_WIDTH - 1:CONV_WIDTH, :] * z
    conv_out = cb_ref[...] * yc
    zlast_ref[...] = zs_sc[tm:tm + pre, :]

    acc = x_ref[...]
    col = 0
    for branch in (att_ref[...], conv_out, y_ref[...]):
        width = branch.shape[-1]
        nb = (_rms(branch) * gb_ref[:, col:col + width]).astype(BF16)
        acc = acc + jnp.dot(nb, wout_ref[col:col + width, :], preferred_element_type=F32)
        col += width
    xo_ref[...] = acc


def _mix_out(att, cb, cc, cx, zstate, y_arr, y_spec, x, gb, wout, cw, cbias, tm, halo, row_stride, tiles_per_seq,
             zstate_spec):
    n = x.shape[0]
    pre = zstate_spec.block_shape[-2]
    rows = lambda w: pl.BlockSpec((tm, w), lambda i: (i, 0))
    in_specs = [rows(W_ATT), rows(W_CONV), rows(W_CONV), rows(W_CONV)]
    args = [att, cb, cc, cx]
    if halo:
        per = tm // SUBLANES
        halo_spec = pl.BlockSpec((SUBLANES, W_CONV), lambda i: (jnp.maximum(i * per - 1, 0), 0))
        in_specs += [halo_spec, halo_spec]
        args += [cc, cx]
    in_specs += [zstate_spec, y_spec, rows(D_MODEL), _full(gb.shape), _full(wout.shape), _full(cw.shape),
                 _full(cbias.shape)]
    args += [zstate, y_arr, x, gb, wout, cw, cbias]
    n_tiles = n // tm
    return pl.pallas_call(
        functools.partial(_mix_out_kernel, tm=tm, halo=halo, row_stride=row_stride, tiles_per_seq=tiles_per_seq),
        grid=(n_tiles,),
        in_specs=in_specs,
        out_specs=[rows(D_MODEL), pl.BlockSpec((pre, W_CONV), lambda i: (i, 0))],
        out_shape=[jax.ShapeDtypeStruct((n, D_MODEL), F32), jax.ShapeDtypeStruct((n_tiles * pre, W_CONV), F32)],
        scratch_shapes=[pltpu.VMEM((tm + pre, W_CONV), F32)],
        compiler_params=_cparams(("parallel",)),
    )(*args)


def _ffn_kernel(*refs, moe, final):
    refs = list(refs)
    x_ref, g_ref = refs[:2]
    pos = 2
    if moe:
        wr_ref, br_ref = refs[pos:pos + 2]
        pos += 2
    wg_ref, wu_ref, wd_ref = refs[pos:pos + 3]
    pos += 3
    if final:
        gf_ref = refs[pos]
        pos += 1
    o_ref, h_sc, acc_sc = refs[pos:pos + 3]
    pos += 3
    if moe:
        comb_sc = refs[pos]
    e = pl.program_id(1)
    c = pl.program_id(2)
    last = (e == pl.num_programs(1) - 1) & (c == pl.num_programs(2) - 1)

    @pl.when((e == 0) & (c == 0))
    def _():
        h = _rms(x_ref[...]) * g_ref[...]
        h_sc[...] = h.astype(BF16)
        acc_sc[...] = jnp.zeros_like(acc_sc)
        if moe:
            logits = jnp.dot(h, wr_ref[...], precision=HIGHEST, preferred_element_type=F32) + br_ref[...]
            lane = lax.broadcasted_iota(jnp.int32, logits.shape, 1)
            logits = jnp.where(lane < N_EXPERTS, logits, -jnp.inf)
            keep = _top_k_mask(logits, 0, N_EXPERTS, EXPERT_TOP_K) & (lane < N_EXPERTS)
            top = jnp.max(logits, axis=1, keepdims=True)
            w = jnp.where(keep, jnp.exp(logits - top), 0.0)
            comb_sc[...] = w / jnp.sum(w, axis=1, keepdims=True)

    hb = h_sc[...]
    a = jnp.dot(hb, wg_ref[...], preferred_element_type=F32)
    b = jnp.dot(hb, wu_ref[...], preferred_element_type=F32)
    t = (a * jax.nn.sigmoid(a) * b).astype(BF16)
    d = jnp.dot(t, wd_ref[...], preferred_element_type=F32)
    if moe:
        d = d * _lane_column(comb_sc[...], e)
    acc_sc[...] += d

    @pl.when(last)
    def _():
        out = x_ref[...] + acc_sc[...]
        if final:
            out = _rms(out) * gf_ref[...]
        o_ref[...] = out


def _ffn(x, g, wg, wu, wd, tm, f_chunk, router=None, g_final=None):
    n = x.shape[0]
    n_exp, _, f_dim = wg.shape
    moe = router is not None
    final = g_final is not None
    rows = pl.BlockSpec((tm, D_MODEL), lambda i, e, c: (i, 0))
    const = lambda shape: pl.BlockSpec(shape, lambda i, e, c: (0,) * len(shape))
    in_specs = [rows, const((1, D_MODEL))]
    args = [x, g]
    if moe:
        in_specs += [const(router[0].shape), const(router[1].shape)]
        args += list(router)
    in_specs += [pl.BlockSpec((None, D_MODEL, f_chunk), lambda i, e, c: (e, 0, c)),
                 pl.BlockSpec((None, D_MODEL, f_chunk), lambda i, e, c: (e, 0, c)),
                 pl.BlockSpec((None, f_chunk, D_MODEL), lambda i, e, c: (e, c, 0))]
    args += [wg, wu, wd]
    if final:
        in_specs.append(const((1, D_MODEL)))
        args.append(g_final)
    scratch = [pltpu.VMEM((tm, D_MODEL), BF16), pltpu.VMEM((tm, D_MODEL), F32)]
    if moe:
        scratch.append(pltpu.VMEM((tm, LANES), F32))
    return pl.pallas_call(
        functools.partial(_ffn_kernel, moe=moe, final=final),
        grid=(n // tm, n_exp, f_dim // f_chunk),
        in_specs=in_specs,
        out_specs=rows,
        out_shape=jax.ShapeDtypeStruct((n, D_MODEL), F32),
        scratch_shapes=scratch,
        compiler_params=_cparams(("parallel", "arbitrary", "arbitrary")),
    )(*args)


MOE_TILE = 1024
MOE_CAP = 320


def _router_kernel(x_ref, g_ref, wr_ref, br_ref, h_ref, comb_ref, cnt_ref):
    h = _rms(x_ref[...]) * g_ref[...]
    h_ref[...] = h.astype(BF16)
    logits = jnp.dot(h, wr_ref[...], precision=HIGHEST, preferred_element_type=F32) + br_ref[...]
    lane = lax.broadcasted_iota(jnp.int32, logits.shape, 1)
    logits = jnp.where(lane < N_EXPERTS, logits, -jnp.inf)
    keep = _top_k_mask(logits, 0, N_EXPERTS, EXPERT_TOP_K) & (lane < N_EXPERTS)
    w = jnp.where(keep, jnp.exp(logits - jnp.max(logits, axis=1, keepdims=True)), 0.0)
    comb = w / jnp.sum(w, axis=1, keepdims=True)
    comb_ref[...] = comb
    count = jnp.sum(jnp.where(comb > 0.0, 1.0, 0.0), axis=0, keepdims=True)
    cnt_ref[...] = jnp.broadcast_to(count, cnt_ref.shape)


def _router(x, g, wr, br, tm):
    n = x.shape[0]
    n_tiles = n // tm
    h, comb, cnt = pl.pallas_call(
        _router_kernel,
        grid=(n_tiles,),
        in_specs=[pl.BlockSpec((tm, D_MODEL), lambda i: (i, 0)), _full((1, D_MODEL)), _full(wr.shape), _full(br.shape)],
        out_specs=[pl.BlockSpec((tm, D_MODEL), lambda i: (i, 0)), pl.BlockSpec((tm, LANES), lambda i: (i, 0)),
                   pl.BlockSpec((SUBLANES, LANES), lambda i: (i, 0))],
        out_shape=[jax.ShapeDtypeStruct((n, D_MODEL), BF16), jax.ShapeDtypeStruct((n, LANES), F32),
                   jax.ShapeDtypeStruct((n_tiles * SUBLANES, LANES), F32)],
        compiler_params=_cparams(("parallel",)),
    )(x, g, wr, br)
    counts = cnt.reshape(n_tiles, SUBLANES, LANES)[:, 0, :N_EXPERTS].astype(jnp.int32).reshape(-1)
    return h, comb, counts


def _moe_kernel(cnt_ref, x_ref, h_ref, comb_ref, wg_ref, wu_ref, wd_ref, gf_ref, o_ref,
                tri_sc, rank_sc, maskt_sc, rankt_sc, *, tm, cap, final):
    i = pl.program_id(0)
    e = pl.program_id(1)
    cap_pad = -(-cap // LANES) * LANES
    sub = 256

    @pl.when((i == 0) & (e == 0))
    def _():
        for rc in range(tm // sub):
            r = lax.broadcasted_iota(jnp.int32, (sub, tm), 0) + rc * sub
            c = lax.broadcasted_iota(jnp.int32, (sub, tm), 1)
            tri_sc[rc * sub:(rc + 1) * sub, :] = jnp.where(c < r, 1.0, 0.0).astype(BF16)

    @pl.when(e == 0)
    def _():
        routed = jnp.where(comb_ref[...] > 0.0, 1.0, 0.0).astype(BF16)
        rank_sc[...] = jnp.dot(tri_sc[...], routed, preferred_element_type=F32)
        er = lax.broadcasted_iota(jnp.int32, (LANES, LANES), 0)
        ec = lax.broadcasted_iota(jnp.int32, (LANES, LANES), 1)
        eye = jnp.where(er == ec, 1.0, 0.0).astype(BF16)
        routed_t = lax.dot_general(eye, routed, NT_DIMS, preferred_element_type=F32)
        rank_t = lax.dot_general(routed_t.astype(BF16), tri_sc[...], NT_DIMS, preferred_element_type=F32)
        maskt_sc[...] = routed_t[0:SUBLANES]
        rankt_sc[...] = rank_t[0:SUBLANES]
        o_ref[...] = x_ref[...]

    count = cnt_ref[i * N_EXPERTS + e]
    gate_col = _lane_column(comb_ref[...], e)
    rank_col = _lane_column(rank_sc[...], e)
    routed_row = maskt_sc[pl.ds(e, 1), :]
    rank_row = rankt_sc[pl.ds(e, 1), :]
    for ch in range(-(-tm // cap)):
        @pl.when(count > ch * cap)
        def _(ch=ch):
            slot = lax.broadcasted_iota(jnp.int32, (cap, tm), 0).astype(F32) + float(ch * cap)
            pick = jnp.where((rank_row == slot) & (routed_row > 0.5), 1.0, 0.0).astype(BF16)
            xg = jnp.dot(pick, h_ref[...], preferred_element_type=F32).astype(BF16)
            a = jnp.dot(xg, wg_ref[...], preferred_element_type=F32)
            b = jnp.dot(xg, wu_ref[...], preferred_element_type=F32)
            t = (a * jax.nn.sigmoid(a) * b).astype(BF16)
            y = jnp.dot(t, wd_ref[...], preferred_element_type=F32).astype(BF16)
            if cap_pad > cap:
                y = jnp.concatenate([y, jnp.zeros((cap_pad - cap, D_MODEL), BF16)], axis=0)
            slot_l = lax.broadcasted_iota(jnp.int32, (tm, cap_pad), 1).astype(F32) + float(ch * cap)
            place = jnp.where((rank_col == slot_l) & (gate_col > 0.0), 1.0, 0.0).astype(BF16)
            o_ref[...] += gate_col * jnp.dot(place, y, preferred_element_type=F32)

    if final:
        @pl.when(e == pl.num_programs(1) - 1)
        def _():
            o_ref[...] = _rms(o_ref[...]) * gf_ref[...]


def _moe_sparse(x, g, wg, wu, wd, router, g_final=None):
    n = x.shape[0]
    tm, cap = MOE_TILE, MOE_CAP
    assert n % tm == 0 and cap % (2 * SUBLANES) == 0 and tm % 256 == 0
    n_exp, _, f_dim = wg.shape
    h, comb, counts = _router(x, g, router[0], router[1], tm)
    final = g_final is not None
    gf = g_final if final else jnp.ones((1, D_MODEL), F32)
    rows = lambda w: pl.BlockSpec((tm, w), lambda i, e, cnt: (i, 0))
    grid_spec = pltpu.PrefetchScalarGridSpec(
        num_scalar_prefetch=1,
        grid=(n // tm, n_exp),
        in_specs=[rows(D_MODEL), rows(D_MODEL), rows(LANES),
                  pl.BlockSpec((None, D_MODEL, f_dim), lambda i, e, cnt: (e, 0, 0)),
                  pl.BlockSpec((None, D_MODEL, f_dim), lambda i, e, cnt: (e, 0, 0)),
                  pl.BlockSpec((None, f_dim, D_MODEL), lambda i, e, cnt: (e, 0, 0)),
                  pl.BlockSpec((1, D_MODEL), lambda i, e, cnt: (0, 0))],
        out_specs=rows(D_MODEL),
        scratch_shapes=[pltpu.VMEM((tm, tm), BF16),
                        pltpu.VMEM((tm, LANES), F32),
                        pltpu.VMEM((SUBLANES, tm), F32),
                        pltpu.VMEM((SUBLANES, tm), F32)],
    )
    return pl.pallas_call(
        functools.partial(_moe_kernel, tm=tm, cap=cap, final=final),
        grid_spec=grid_spec,
        out_shape=jax.ShapeDtypeStruct((n, D_MODEL), F32),
        compiler_params=_cparams(("arbitrary", "arbitrary")),
    )(counts, x, h, comb, wg, wu, wd, gf)


def _alibi_slopes_np():
    return (2.0 ** (-8.0 * np.arange(1, N_HEADS + 1) / N_HEADS)).astype(np.float32)


def _ssm_params(a_re, a_im, log_dt, b_re, b_im, c_re, c_im):
    ar, ai = a_re.astype(F32), a_im.astype(F32)
    dt = jnp.exp(log_dt.astype(F32))[:, None]
    mag = jnp.exp(dt * ar)
    abr, abi = mag * jnp.cos(dt * ai), mag * jnp.sin(dt * ai)
    den = ar * ar + ai * ai
    zr = ((abr - 1.0) * ar + abi * ai) / den
    zi = (abi * ar - (abr - 1.0) * ai) / den
    br, bim = b_re.astype(F32), b_im.astype(F32)
    bbr = zr[..., None] * br - zi[..., None] * bim
    bbi = zr[..., None] * bim + zi[..., None] * br
    eye = jnp.eye(N_GROUPS, dtype=F32)
    to_in = lambda m: jnp.einsum('gnc,gh->gchn', m, eye).reshape(W_SSM, N_STATE)
    to_out = lambda m: jnp.einsum('gcn,gh->gnhc', m, eye).reshape(N_STATE, W_SSM)
    bbd = jnp.concatenate([to_in(bbr), to_in(bbi)], axis=1).astype(BF16)
    cbd = jnp.concatenate([to_out(c_re.astype(F32)), -to_out(c_im.astype(F32))], axis=0).astype(BF16)
    a = jnp.stack([abr.reshape(N_STATE), abi.reshape(N_STATE)])
    return a, bbd, cbd


def _trunk(x, w, depth, bsz, seq, time_major, attend, conv0, h0r, h0i):
    n = bsz * seq
    ks, vs, zs, hrs, his = [], [], [], [], []
    if time_major:
        tm, tiles_per_seq, row_stride, halo = n, 1, bsz, False
        pre = (CONV_WIDTH - 1) * bsz
        u_shape = (n, W_SSM)
        u_spec = pl.BlockSpec((tm, W_SSM), lambda i: (0, 0))
        kv_shape = (n, W_ATT)
        kv_spec = pl.BlockSpec((tm, W_ATT), lambda i: (0, 0))
        zstate_spec = pl.BlockSpec((pre, W_CONV), lambda i: (0, 0))
        t_chunk = seq
    else:
        tm, row_stride, halo = ROW_TILE, 1, True
        tiles_per_seq = seq // tm
        pre = SUBLANES
        u_shape = (seq, bsz * W_SSM)
        u_spec = pl.BlockSpec((tm, W_SSM), lambda i: (i % tiles_per_seq, i // tiles_per_seq))
        kv_shape = (depth, bsz, W_ATT, seq)
        zstate_spec = pl.BlockSpec((None, pre, W_CONV), lambda i: (i // tiles_per_seq, 0, 0))
        t_chunk = 32
    k = v = None
    for l in range(depth):
        if not time_major:
            kv_spec = pl.BlockSpec((None, None, W_ATT, tm),
                                   lambda i, l=l: (l, i // tiles_per_seq, 0, i % tiles_per_seq))
        q, k, v, cb, cc, cx, u = _mix_in(x, w['g_mix'][l], w['w_in'][l], w['w_kvt'][l], tm, u_shape, u_spec,
                                         kv_shape, kv_spec, kv_prev=None if time_major or l == 0 else (k, v))
        att = attend(l, q, k, v)
        y, hr, hi = _ssm(u, h0r[l], h0i[l], *w['ssm'][l], bsz, seq, t_chunk)
        x, zlast = _mix_out(att, cb, cc, cx, conv0[l], y, u_spec, x, w['g_branch'][l], w['w_out'][l],
                            w['conv_w'][l], w['conv_b'][l], tm, halo, row_stride, tiles_per_seq, zstate_spec)
        g_final = w['g_final'] if l == depth - 1 else None
        if l % 2 == 0:
            x = _ffn(x, w['g_ffn'][l], *w['dense'][l // 2], tm, w['dense'][l // 2][0].shape[-1] // 2, g_final=g_final)
        else:
            wg, wu, wd, router = w['moe'][l // 2]
            if x.shape[0] % MOE_TILE == 0:
                x = _moe_sparse(x, w['g_ffn'][l], wg, wu, wd, router, g_final=g_final)
            else:
                x = _ffn(x, w['g_ffn'][l], wg, wu, wd, tm, wg.shape[-1], router=router, g_final=g_final)
        ks.append(k)
        vs.append(v)
        zs.append(zlast)
        hrs.append(hr)
        his.append(hi)
    return x, ks, vs, zs, hrs, his


def kernel(x_prompt, x_sample, cache_k, cache_v, state_conv, state_ssm_re, state_ssm_im, page_table, g_mix_norm, w_in, g_branch, w_out, conv_w, conv_b, ssm_a_re, ssm_a_im, ssm_log_dt, ssm_b_re, ssm_b_im, ssm_c_re, ssm_c_im, ssm_d, ssm_w_glu, g_ffn_norm, w_ffn_gate, w_ffn_up, w_ffn_down, w_router, b_router, w_exp_gate, w_exp_up, w_exp_down, g_final):
    depth = w_in.shape[0]
    bp, sp, _ = x_prompt.shape
    bs, ss, _ = x_sample.shape
    n_pool, page = cache_k.shape[1], cache_k.shape[2]
    slopes_np = _alibi_slopes_np()
    slopes = jnp.asarray(slopes_np)

    row = lambda a: a.reshape(a.shape[0], 1, a.shape[-1])
    w_in_bf16 = w_in.astype(BF16)
    w = {
        'g_mix': row(g_mix_norm), 'w_in': w_in_bf16, 'g_branch': row(g_branch), 'w_out': w_out.astype(BF16),
        'w_kvt': jnp.swapaxes(w_in_bf16[:, :, W_ATT:3 * W_ATT], 1, 2),
        'conv_w': conv_w, 'conv_b': row(conv_b), 'g_ffn': row(g_ffn_norm), 'g_final': g_final.reshape(1, D_MODEL),
        'ssm': [(*_ssm_params(ssm_a_re[l], ssm_a_im[l], ssm_log_dt[l], ssm_b_re[l], ssm_b_im[l], ssm_c_re[l],
                              ssm_c_im[l]), ssm_d[l].reshape(1, W_SSM), ssm_w_glu[l].astype(BF16))
                for l in range(depth)],
        'dense': [(w_ffn_gate[j:j + 1].astype(BF16), w_ffn_up[j:j + 1].astype(BF16), w_ffn_down[j:j + 1].astype(BF16))
                  for j in range(w_ffn_gate.shape[0])],
        'moe': [(w_exp_gate[j].astype(BF16), w_exp_up[j].astype(BF16), w_exp_down[j].astype(BF16),
                 (jnp.pad(w_router[j], ((0, 0), (0, LANES - N_EXPERTS))),
                  jnp.pad(b_router[j].reshape(1, N_EXPERTS), ((0, 0), (0, LANES - N_EXPERTS)))))
                for j in range(w_exp_gate.shape[0])],
    }

    conv0_p = jnp.zeros((depth, bp, SUBLANES, W_CONV), F32)
    h0_p = jnp.zeros((depth, bp, N_STATE), F32)
    attend_p = lambda l, q, kt, vt: _moba_prompt(q, kt, vt, l, slopes, bp, sp)
    yp, kts, vts, zs, hrs, his = _trunk(x_prompt.reshape(bp * sp, D_MODEL), w, depth, bp, sp, False, attend_p,
                                        conv0_p, h0_p, h0_p)
    y_prompt = yp.reshape(bp, sp, D_MODEL)
    from_t = lambda a: jnp.transpose(a[-1].reshape(depth, bp, N_HEADS, HEAD_DIM, sp), (0, 1, 4, 2, 3))
    k_p, v_p = from_t(kts), from_t(vts)
    tiles = sp // ROW_TILE
    conv_p = jnp.stack(zs).reshape(depth, bp, tiles, SUBLANES, W_CONV)[:, :, -1, SUBLANES - (CONV_WIDTH - 1):, :]
    sre_p = jnp.stack(hrs).reshape(depth, bp, N_GROUPS, SSM_STATE)
    sim_p = jnp.stack(his).reshape(depth, bp, N_GROUPS, SSM_STATE)

    to_t = lambda c: jnp.transpose(c, (0, 1, 3, 4, 2)).reshape(depth, n_pool, W_ATT, page)
    cache_kt, cache_vt = to_t(cache_k), to_t(cache_v)
    xs = jnp.swapaxes(x_sample, 0, 1).reshape(ss * bs, D_MODEL)
    conv0_s = jnp.swapaxes(state_conv, 1, 2).reshape(depth, (CONV_WIDTH - 1) * bs, W_CONV)
    h0r_s = state_ssm_re.reshape(depth, bs, N_STATE)
    h0i_s = state_ssm_im.reshape(depth, bs, N_STATE)
    attend_s = lambda l, q, k, v: _moba_sample(q, k, v, cache_kt, cache_vt, l, page_table, slopes_np, bs, ss)
    ys, ks, vs, zs, hrs, his = _trunk(xs, w, depth, bs, ss, True, attend_s, conv0_s, h0r_s, h0i_s)
    from_tb = lambda a, width: jnp.swapaxes(a.reshape(-1, ss, bs, width), 1, 2)
    y_sample = from_tb(ys, D_MODEL)[0]
    k_s = from_tb(jnp.stack(ks), W_ATT).reshape(depth, bs, ss, N_HEADS, HEAD_DIM)
    v_s = from_tb(jnp.stack(vs), W_ATT).reshape(depth, bs, ss, N_HEADS, HEAD_DIM)
    conv_s = jnp.swapaxes(jnp.stack(zs).reshape(depth, CONV_WIDTH - 1, bs, W_CONV), 1, 2)
    sre_s = jnp.stack(hrs).reshape(depth, bs, N_GROUPS, SSM_STATE)
    sim_s = jnp.stack(his).reshape(depth, bs, N_GROUPS, SSM_STATE)
    return (y_prompt, y_sample, k_p, v_p, conv_p, sre_p, sim_p, k_s, v_s, conv_s, sre_s, sim_s)
```

```python
import functools

import numpy as np
import jax
import jax.numpy as jnp
from jax import lax
from jax.experimental import pallas as pl
from jax.experimental.pallas import tpu as pltpu

F32 = jnp.float32
BF16 = jnp.bfloat16
HIGHEST = lax.Precision.HIGHEST

D_MODEL = 1024
N_HEADS = 8
HEAD_DIM = 64
W_ATT = N_HEADS * HEAD_DIM
W_CONV = 256
W_SSM = 256
CONV_WIDTH = 3
SSM_GROUP = 16
N_GROUPS = 16
SSM_STATE = 64
N_STATE = N_GROUPS * SSM_STATE
MOBA_BLOCK = 256
MOBA_TOP_K = 3
ATT_SCALE = HEAD_DIM ** -0.5
N_EXPERTS = 8
EXPERT_TOP_K = 2
RMS_EPS = 1e-6
NEG = -1e30

LANES = 128
SUBLANES = 8
VMEM_LIMIT = 56 * 1024 * 1024
ROW_TILE = 512
NT_DIMS = (((1,), (1,)), ((), ()))


def _cparams(sem):
    return pltpu.CompilerParams(dimension_semantics=sem, vmem_limit_bytes=VMEM_LIMIT)


def _rms(x):
    return x * lax.rsqrt(jnp.mean(x * x, axis=-1, keepdims=True) + RMS_EPS)


def _full(shape):
    n = len(shape)
    return pl.BlockSpec(shape, lambda *_: (0,) * n)


def _split_bf16(x):
    hi = x.astype(BF16)
    return hi, (x - hi.astype(F32)).astype(BF16)


def _dot_split(a, b):
    a_hi, a_lo = _split_bf16(a)
    b_hi, b_lo = _split_bf16(b)
    dot = functools.partial(jnp.dot, preferred_element_type=F32)
    return dot(a_hi, b_hi) + (dot(a_hi, b_lo) + dot(a_lo, b_hi))


def _lane_column(x, idx):
    lane = lax.broadcasted_iota(jnp.int32, x.shape, 1)
    return jnp.sum(jnp.where(lane == idx, x, 0.0), axis=1, keepdims=True)


def _top_k_mask(score, first, n_cand, k):
    lane = lax.broadcasted_iota(jnp.int32, score.shape, 1)
    rank = jnp.zeros(score.shape, jnp.int32)
    for m in range(first, first + n_cand):
        sm = score[:, m:m + 1]
        rank += ((sm > score) | ((sm == score) & (m < lane))).astype(jnp.int32)
    return rank < k


def _mix_in_kernel(x_ref, g_ref, w_ref, wkvt_ref, *rest, kv_transposed):
    q_ref, k_ref, v_ref, cb_ref, cc_ref, cx_ref, u_ref = rest[-7:]
    h = (_rms(x_ref[...]) * g_ref[...]).astype(BF16)
    outs = (q_ref, k_ref, v_ref, cb_ref, cc_ref, cx_ref, u_ref)
    widths = (W_ATT,) * 3 + (W_CONV,) * 3 + (W_SSM,)
    col = 0
    for idx, (ref, width) in enumerate(zip(outs, widths)):
        if kv_transposed and idx in (1, 2):
            ref[...] = lax.dot_general(wkvt_ref[(idx - 1) * W_ATT:idx * W_ATT, :], h, NT_DIMS,
                                       preferred_element_type=F32)
        else:
            ref[...] = jnp.dot(h, w_ref[:, col:col + width], preferred_element_type=F32)
        col += width


def _mix_in(x, g, w_bf16, wkvt_bf16, tm, u_shape, u_spec, kv_shape, kv_spec, kv_prev=None):
    n = x.shape[0]
    rows = lambda w: pl.BlockSpec((tm, w), lambda i: (i, 0))
    conv_widths = (W_CONV, W_CONV, W_CONV)
    in_specs = [rows(D_MODEL), _full((1, D_MODEL)), _full(w_bf16.shape), _full(wkvt_bf16.shape)]
    args = [x, g, w_bf16, wkvt_bf16]
    aliases = {}
    if kv_prev is not None:
        in_specs += [pl.BlockSpec(memory_space=pl.ANY)] * 2
        args += list(kv_prev)
        aliases = {4: 1, 5: 2}
    return pl.pallas_call(
        functools.partial(_mix_in_kernel, kv_transposed=len(kv_shape) == 4),
        grid=(n // tm,),
        in_specs=in_specs,
        out_specs=[rows(W_ATT), kv_spec, kv_spec] + [rows(w) for w in conv_widths] + [u_spec],
        out_shape=[jax.ShapeDtypeStruct((n, W_ATT), F32), jax.ShapeDtypeStruct(kv_shape, F32),
                   jax.ShapeDtypeStruct(kv_shape, F32)]
                  + [jax.ShapeDtypeStruct((n, w), F32) for w in conv_widths] + [jax.ShapeDtypeStruct(u_shape, F32)],
        input_output_aliases=aliases,
        compiler_params=_cparams(("parallel",)),
    )(*args)


AUG_KPOS = 32


def _moba_prompt_consts(seq):
    n_blocks = seq // MOBA_BLOCK
    kc = np.zeros((2, 2 * HEAD_DIM, seq), np.float32)
    vc = np.zeros((2, 2 * HEAD_DIM, seq), np.float32)
    for hl in range(2):
        aug = HEAD_DIM * (1 - hl)
        for n in range(n_blocks):
            kc[hl, aug + n, n * MOBA_BLOCK:(n + 1) * MOBA_BLOCK] = 1.0
        kc[hl, aug + AUG_KPOS, :] = np.arange(seq) % MOBA_BLOCK
        vc[hl, aug, :] = 1.0
    return jnp.asarray(kc), jnp.asarray(vc)


def _moba_prompt_kernel(slopes_ref, q_ref, kt_ref, vt_ref, kc_ref, vc_ref, o_ref, kb_sc, vb_sc, *, n_blocks):
    j = pl.program_id(1)
    blk = MOBA_BLOCK
    pair = 2 * HEAD_DIM
    kt = kt_ref[...]
    vt = vt_ref[...]
    row = lax.broadcasted_iota(jnp.int32, kt.shape, 0)
    for hl in range(2):
        own_rows = (row >= HEAD_DIM * hl) & (row < HEAD_DIM * (hl + 1))
        kb_sc[hl] = jnp.where(own_rows, kt, kc_ref[hl]).astype(BF16)
        vb_sc[hl] = jnp.where(own_rows, vt, vc_ref[hl]).astype(BF16)

    krow = lax.broadcasted_iota(jnp.int32, (pair, LANES), 0)
    klane = lax.broadcasted_iota(jnp.int32, (pair, LANES), 1)
    aug_of_row = jnp.where(krow < HEAD_DIM, HEAD_DIM, 0)
    kmean = jnp.zeros((pair, LANES), F32)
    for n in range(n_blocks):
        mean_n = jnp.mean(kt[:, n * blk:(n + 1) * blk], axis=1, keepdims=True)
        kmean = jnp.where(klane == aug_of_row + n, mean_n, kmean)
    kmean_hi, kmean_lo = _split_bf16(kmean)

    lane = lax.broadcasted_iota(jnp.int32, (blk, pair), 1)
    r_io = lax.broadcasted_iota(jnp.int32, (blk, blk), 0)
    c_io = lax.broadcasted_iota(jnp.int32, (blk, blk), 1)
    causal = c_io <= r_io
    for i in range(n_blocks):
        q_i = q_ref[i * blk:(i + 1) * blk, :]
        past, width = i * blk, (i + 1) * blk
        if i > MOBA_TOP_K:
            q_hi, q_lo = _split_bf16(q_i)
            dot = functools.partial(jnp.dot, preferred_element_type=F32)
            gate = dot(q_hi, kmean_hi) + (dot(q_hi, kmean_lo) + dot(q_lo, kmean_hi))
        outs = []
        for hl in range(2):
            aug = HEAD_DIM * (1 - hl)
            slope = slopes_ref[2 * j + hl]
            qh = jnp.where((lane >= HEAD_DIM * hl) & (lane < HEAD_DIM * (hl + 1)), q_i, 0.0)
            is_past = (lane >= aug) & (lane < aug + i)
            if i > MOBA_TOP_K:
                keep =_top_k_mask(jnp.where(is_past, gate, -jnp.inf), aug, i, MOBA_TOP_K) & is_past
            else:
                keep = is_past
            bias = jnp.where(keep | (lane == aug + i), 0.0, NEG) + slope * (blk * (lane - (aug + i)).astype(F32))
            q_aug = jnp.where((lane >= aug) & (lane <= aug + i), bias, qh * ATT_SCALE)
            q_aug = jnp.where(lane == aug + AUG_KPOS, slope, q_aug).astype(BF16)

            s = jnp.dot(q_aug, kb_sc[hl, :, 0:width], preferred_element_type=F32)
            s_own = jnp.where(causal, s[:, past:width], NEG)
            m = jnp.max(s_own, axis=1, keepdims=True)
            if i > 0:
                m = jnp.maximum(m, jnp.max(s[:, 0:past], axis=1, keepdims=True))
            o = lax.dot_general(jnp.exp(s_own - m).astype(BF16), vb_sc[hl, :, past:width], NT_DIMS,
                                preferred_element_type=F32)
            if i > 0:
                o = o + lax.dot_general(jnp.exp(s[:, 0:past] - m).astype(BF16), vb_sc[hl, :, 0:past], NT_DIMS,
                                        preferred_element_type=F32)
            outs.append(o * (1.0 / o[:, aug:aug + 1]))
        o_ref[i * blk:(i + 1) * blk, :] = jnp.where(lane < HEAD_DIM, outs[0], outs[1])


def _moba_prompt(q, kt, vt, layer, slopes, bsz, seq):
    assert seq % MOBA_BLOCK == 0
    n_blocks = seq // MOBA_BLOCK
    assert n_blocks <= AUG_KPOS < HEAD_DIM
    pair = 2 * HEAD_DIM
    kc, vc = _moba_prompt_consts(seq)
    kv_spec = pl.BlockSpec((None, None, pair, seq), lambda b, j: (layer, b, j, 0))
    qo_spec = pl.BlockSpec((None, seq, pair), lambda b, j: (b, 0, j))
    const_spec = pl.BlockSpec((2, pair, seq), lambda b, j: (0, 0, 0))
    out = pl.pallas_call(
        functools.partial(_moba_prompt_kernel, n_blocks=n_blocks),
        grid=(bsz, W_ATT // pair),
        in_specs=[pl.BlockSpec(memory_space=pltpu.SMEM), qo_spec, kv_spec, kv_spec, const_spec, const_spec],
        out_specs=qo_spec,
        out_shape=jax.ShapeDtypeStruct((bsz, seq, W_ATT), F32),
        scratch_shapes=[pltpu.VMEM((2, pair, seq), BF16), pltpu.VMEM((2, pair, seq), BF16)],
        compiler_params=_cparams(("parallel", "parallel")),
    )(slopes, q.reshape(bsz, seq, W_ATT), kt, vt, kc, vc)
    return out.reshape(bsz * seq, W_ATT)


PAGES_PER_STEP = 32


def _moba_sample_kernel(pt_ref, slope_ref, tq_ref, q_ref, kn_ref, vn_ref, *rest, n_pages, page, t_new):
    del pt_ref
    k_refs = rest[:PAGES_PER_STEP]
    v_refs = rest[PAGES_PER_STEP:2 * PAGES_PER_STEP]
    o_ref, st_sc, selb_sc, ksum_sc, qbd_sc, qbdf_sc, own_sc, inv_sc, acc_sc = rest[2 * PAGES_PER_STEP:]
    ph = pl.program_id(1)
    pg = pl.program_id(2)
    n_groups = n_pages // PAGES_PER_STEP
    ppb = MOBA_BLOCK // page
    n_blocks = n_pages // ppb
    past = n_pages * page
    ht = N_HEADS * t_new

    @pl.when((ph == 0) & (pg == 0))
    def _():
        qt = jnp.concatenate([q_ref[...]] * N_HEADS, axis=0)
        row = lax.broadcasted_iota(jnp.int32, (ht, W_ATT), 0)
        col = lax.broadcasted_iota(jnp.int32, (ht, W_ATT), 1)
        qbd = jnp.where(col // HEAD_DIM == row // t_new, qt, 0.0) * ATT_SCALE
        qbdf_sc[...] = qbd
        qbd_sc[...] = qbd.astype(BF16)
        ksum_sc[...] = jnp.zeros_like(ksum_sc)

    @pl.when(ph == 0)
    def _():
        lane = lax.broadcasted_iota(jnp.int32, (W_ATT, LANES), 1)
        for bb in range(PAGES_PER_STEP // ppb):
            tot = None
            for pp in range(ppb):
                ii = bb * ppb + pp
                kp = k_refs[ii][...]
                tot = kp if tot is None else tot + kp
                st_sc[pg * PAGES_PER_STEP + ii] = jnp.dot(qbd_sc[...], kp.astype(BF16), preferred_element_type=F32)
            blk_idx = pg * (PAGES_PER_STEP // ppb) + bb
            ksum_sc[...] = jnp.where(lane == blk_idx, jnp.sum(tot, axis=1, keepdims=True), ksum_sc[...])

    @pl.when((ph == 0) & (pg == n_groups - 1))
    def _():
        slope = slope_ref[...]
        tq = tq_ref[...]
        lane = lax.broadcasted_iota(jnp.int32, (ht, LANES), 1)
        lane_f = lane.astype(F32)
        gate = jnp.dot(qbdf_sc[...], ksum_sc[...], precision=HIGHEST, preferred_element_type=F32)
        gate = jnp.where(lane < n_blocks, gate, -jnp.inf)
        keep = _top_k_mask(gate, 0, n_blocks, min(MOBA_TOP_K, n_blocks)) & (lane < n_blocks)
        sel_bias = jnp.where(keep, 0.0, NEG)
        for n in range(n_blocks):
            selb_sc[n] = jnp.broadcast_to(sel_bias[:, n:n + 1], (ht, LANES))

        k_own = jnp.concatenate([kn_ref[...], jnp.zeros((LANES - t_new, W_ATT), F32)], axis=0)
        so = lax.dot_general(qbd_sc[...], k_own.astype(BF16), NT_DIMS, preferred_element_type=F32)
        lo = jnp.where((lane_f <= tq) & (lane < t_new), so - slope * (tq - lane_f), NEG)
        qpos = tq + float(past)

        def logits(n, pp):
            pidx = n * ppb + pp
            dist = qpos - (lane_f + jnp.asarray(pidx * page, F32))
            return pidx, st_sc[pidx] - slope * dist + selb_sc[n]

        def max_body(n, m_vec):
            for pp in range(ppb):
                m_vec = jnp.maximum(m_vec, logits(n, pp)[1])
            return m_vec

        m = jnp.max(lax.fori_loop(0, n_blocks, max_body, lo, unroll=4), axis=1, keepdims=True)
        eo = jnp.exp(lo - m)

        def exp_body(n, l_vec):
            for pp in range(ppb):
                pidx, lg = logits(n, pp)
                e = jnp.exp(lg - m)
                st_sc[pidx] = e
                l_vec = l_vec + e
            return l_vec

        l_vec = lax.fori_loop(0, n_blocks, exp_body, eo, unroll=4)
        inv = 1.0 / jnp.sum(l_vec, axis=1, keepdims=True)
        inv_sc[...] = jnp.broadcast_to(inv, inv_sc.shape)
        own_sc[...] = eo * inv
        acc_sc[...] = jnp.zeros_like(acc_sc)

    @pl.when(ph == 1)
    def _():
        inv = inv_sc[...]
        tot = None
        for ii in range(PAGES_PER_STEP):
            p = (st_sc[pg * PAGES_PER_STEP + ii] * inv).astype(BF16)
            d = lax.dot_general(p, v_refs[ii][...].astype(BF16), NT_DIMS, preferred_element_type=F32)
            tot = d if tot is None else tot + d
        acc_sc[...] += tot

    @pl.when((ph == 1) & (pg == n_groups - 1))
    def _():
        v_own = jnp.concatenate([vn_ref[...], jnp.zeros((LANES - t_new, W_ATT), F32)], axis=0)
        acc = acc_sc[...] + jnp.dot(own_sc[...].astype(BF16), v_own.astype(BF16), preferred_element_type=F32)
        col = lax.broadcasted_iota(jnp.int32, (t_new, W_ATT), 1)
        out = jnp.zeros((t_new, W_ATT), F32)
        for h in range(N_HEADS):
            out += jnp.where(col // HEAD_DIM == h, acc[h * t_new:(h + 1) * t_new, :], 0.0)
        o_ref[...] = out


def _moba_sample(q, k, v, cache_kt, cache_vt, layer, page_table, slopes_np, bsz, t_new):
    page = cache_kt.shape[-1]
    n_pages = page_table.shape[1]
    assert MOBA_BLOCK % page == 0 and (n_pages * page) % MOBA_BLOCK == 0 and n_pages % PAGES_PER_STEP == 0
    assert t_new % SUBLANES == 0 and t_new <= LANES and page == LANES
    ht = N_HEADS * t_new
    n_groups = n_pages // PAGES_PER_STEP
    n_blocks = n_pages * page // MOBA_BLOCK
    assert n_blocks <= LANES
    row_h = np.arange(ht) // t_new
    slope_r = jnp.asarray(np.broadcast_to(slopes_np[row_h][:, None], (ht, LANES)), F32)
    tq_r = jnp.asarray(np.broadcast_to((np.arange(ht) % t_new)[:, None], (ht, LANES)), F32)
    wide = lambda a: a.reshape(t_new, bsz * W_ATT)
    tok_spec = pl.BlockSpec((t_new, W_ATT), lambda b, ph, pg, pt: (0, b))
    const_spec = pl.BlockSpec((ht, LANES), lambda b, ph, pg, pt: (0, 0))

    def k_map(ii):
        def f(b, ph, pg, pt):
            p = jnp.where(ph == 0, pg, n_groups - 1) * PAGES_PER_STEP + ii
            return (layer, pt[b * n_pages + p], 0, 0)
        return f

    def v_map(ii):
        def f(b, ph, pg, pt):
            p = jnp.where(ph == 1, pg, 0) * PAGES_PER_STEP + ii
            return (layer, pt[b * n_pages + p], 0, 0)
        return f

    page_block = (None, None, W_ATT, page)
    grid_spec = pltpu.PrefetchScalarGridSpec(
        num_scalar_prefetch=1,
        grid=(bsz, 2, n_groups),
        in_specs=[const_spec, const_spec, tok_spec, tok_spec, tok_spec]
                 + [pl.BlockSpec(page_block, k_map(ii)) for ii in range(PAGES_PER_STEP)]
                 + [pl.BlockSpec(page_block, v_map(ii)) for ii in range(PAGES_PER_STEP)],
        out_specs=tok_spec,
        scratch_shapes=[pltpu.VMEM((n_pages, ht, page), F32),
                        pltpu.VMEM((n_blocks, ht, LANES), F32),
                        pltpu.VMEM((W_ATT, LANES), F32),
                        pltpu.VMEM((ht, W_ATT), BF16),
                        pltpu.VMEM((ht, W_ATT), F32),
                        pltpu.VMEM((ht, LANES), F32),
                        pltpu.VMEM((ht, LANES), F32),
                        pltpu.VMEM((ht, W_ATT), F32)],
    )
    out = pl.pallas_call(
        functools.partial(_moba_sample_kernel, n_pages=n_pages, page=page, t_new=t_new),
        grid_spec=grid_spec,
        out_shape=jax.ShapeDtypeStruct((t_new, bsz * W_ATT), F32),
        compiler_params=_cparams(("parallel", "arbitrary", "arbitrary")),
    )(page_table.reshape(-1), slope_r, tq_r, wide(q), wide(k), wide(v), *([cache_kt] * PAGES_PER_STEP),
      *([cache_vt] * PAGES_PER_STEP))
    return out.reshape(t_new * bsz, W_ATT)


SCAN_LANES = 256


def _gelu_tanh(x):
    return 0.5 * x * (1.0 + jnp.tanh(np.float32(np.sqrt(2.0 / np.pi)) * (x + 0.044715 * (x * x * x))))


def _ssm_kernel(u_ref, h0r_ref, h0i_ref, a_ref, bbd_ref, cbd_ref, d_ref, wglu_ref,
                y_ref, hr_ref, hi_ref, xs_sc, hr_sc, hi_sc, *wide_sc, bsz, t_chunk):
    step = pl.program_id(0)

    @pl.when(step == 0)
    def _():
        hr_sc[...] = h0r_ref[...]
        hi_sc[...] = h0i_ref[...]

    n_half = W_SSM // LANES
    if wide_sc:
        u_scs, y_scs = wide_sc[:n_half], wide_sc[n_half:]
        for b in range(bsz):
            for hf in range(n_half):
                col = b * W_SSM + hf * LANES
                u_scs[hf][pl.ds(b, t_chunk, stride=bsz), :] = u_ref[:, col:col + LANES]
        u = jnp.concatenate([sc[...] for sc in u_scs], axis=1)
    else:
        u = u_ref[...]
    xs_sc[...] = jnp.dot(u.astype(BF16), bbd_ref[...], preferred_element_type=F32)
    for lc in range(N_STATE // SCAN_LANES):
        re = slice(lc * SCAN_LANES, (lc + 1) * SCAN_LANES)
        im = slice(N_STATE + lc * SCAN_LANES, N_STATE + (lc + 1) * SCAN_LANES)
        ar = jnp.broadcast_to(a_ref[0:1, re], (bsz, SCAN_LANES))
        ai = jnp.broadcast_to(a_ref[1:2, re], (bsz, SCAN_LANES))

        def scan_step(t, carry, re=re, im=im, ar=ar, ai=ai):
            hr, hi = carry
            rows = pl.ds(pl.multiple_of(t * bsz, bsz), bsz)
            nr = ar * hr - ai * hi + xs_sc[rows, re]
            ni = ar * hi + ai * hr + xs_sc[rows, im]
            xs_sc[rows, re] = nr
            xs_sc[rows, im] = ni
            return nr, ni

        hr, hi = lax.fori_loop(0, t_chunk, scan_step, (hr_sc[:, re], hi_sc[:, re]))
        hr_sc[:, re] = hr
        hi_sc[:, re] = hi

    y = jnp.dot(xs_sc[...].astype(BF16), cbd_ref[...], preferred_element_type=F32) + d_ref[...] * u
    y = _gelu_tanh(y)
    y = y * jax.nn.sigmoid(jnp.dot(y.astype(BF16), wglu_ref[...], preferred_element_type=F32))
    if wide_sc:
        for hf in range(n_half):
            y_scs[hf][...] = y[:, hf * LANES:(hf + 1) * LANES]
        for b in range(bsz):
            for hf in range(n_half):
                col = b * W_SSM + hf * LANES
                y_ref[:, col:col + LANES] = y_scs[hf][pl.ds(b, t_chunk, stride=bsz), :]
    else:
        y_ref[...] = y

    @pl.when(step == pl.num_programs(0) - 1)
    def _():
        hr_ref[...] = hr_sc[...]
        hi_ref[...] = hi_sc[...]


def _ssm(u, h0r, h0i, a, bbd, cbd, d, wglu, bsz, seq, t_chunk):
    assert bsz % SUBLANES == 0 and seq % t_chunk == 0 and u.shape[0] in (seq, seq * bsz)
    rows = t_chunk * bsz
    wide = u.shape[0] == seq
    blk = pl.BlockSpec((t_chunk, bsz * W_SSM) if wide else (rows, W_SSM), lambda s: (s, 0))
    st_spec = _full((bsz, N_STATE))
    scratch = [pltpu.VMEM((rows, 2 * N_STATE), F32), pltpu.VMEM((bsz, N_STATE), F32), pltpu.VMEM((bsz, N_STATE), F32)]
    if wide:
        scratch += [pltpu.VMEM((rows, LANES), F32)] * (2 * (W_SSM // LANES))
    return pl.pallas_call(
        functools.partial(_ssm_kernel, bsz=bsz, t_chunk=t_chunk),
        grid=(seq // t_chunk,),
        in_specs=[blk, st_spec, st_spec, _full(a.shape), _full(bbd.shape), _full(cbd.shape), _full(d.shape),
                  _full(wglu.shape)],
        out_specs=[blk, st_spec, st_spec],
        out_shape=[jax.ShapeDtypeStruct(u.shape, F32), jax.ShapeDtypeStruct((bsz, N_STATE), F32),
                   jax.ShapeDtypeStruct((bsz, N_STATE), F32)],
        scratch_shapes=scratch,
        compiler_params=_cparams(("arbitrary",)),
    )(u, h0r, h0i, a, bbd, cbd, d, wglu)


def _mix_out_kernel(*refs, tm, halo, row_stride, tiles_per_seq):
    if halo:
        (att_ref, cb_ref, cc_ref, cx_ref, cch_ref, cxh_ref, zst_ref, y_ref, x_ref, gb_ref, wout_ref, cw_ref,
         cbias_ref, xo_ref, zlast_ref, zs_sc) = refs
    else:
        (att_ref, cb_ref, cc_ref, cx_ref, zst_ref, y_ref, x_ref, gb_ref, wout_ref, cw_ref,
         cbias_ref, xo_ref, zlast_ref, zs_sc) = refs
    pre = zs_sc.shape[0] - tm
    z = cc_ref[...] * cx_ref[...]
    if halo:
        first = pl.program_id(0) % tiles_per_seq == 0
        zs_sc[0:pre, :] = jnp.where(first, zst_ref[...], cch_ref[...] * cxh_ref[...])
    else:
        zs_sc[0:pre, :] = zst_ref[...]
    zs_sc[pre:, :] = z
    yc = cbias_ref[...]
    for tap in range(CONV_WIDTH - 1):
        off = pre - (CONV_WIDTH - 1 - tap) * row_stride
        yc = yc + cw_ref[tap:tap + 1, :] * zs_sc[off:off + tm, :]
    yc = yc + cw_ref[CONV_WIDTH - 1:CONV_WIDTH, :] * z
    conv_out = cb_ref[...] * yc
    zlast_ref[...] = zs_sc[tm:tm + pre, :]

    acc = x_ref[...]
    col = 0
    for branch in (att_ref[...], conv_out, y_ref[...]):
        width = branch.shape[-1]
        nb = (_rms(branch) * gb_ref[:, col:col + width]).astype(BF16)
        acc = acc + jnp.dot(nb, wout_ref[col:col + width, :], preferred_element_type=F32)
        col += width
    xo_ref[...] = acc


def _mix_out(att, cb, cc, cx, zstate, y_arr, y_spec, x, gb, wout, cw, cbias, tm, halo, row_stride, tiles_per_seq,
             zstate_spec):
    n = x.shape[0]
    pre = zstate_spec.block_shape[-2]
    rows = lambda w: pl.BlockSpec((tm, w), lambda i: (i, 0))
    in_specs = [rows(W_ATT), rows(W_CONV), rows(W_CONV), rows(W_CONV)]
    args = [att, cb, cc, cx]
    if halo:
        per = tm // SUBLANES
        halo_spec = pl.BlockSpec((SUBLANES, W_CONV), lambda i: (jnp.maximum(i * per - 1, 0), 0))
        in_specs += [halo_spec, halo_spec]
        args += [cc, cx]
    in_specs += [zstate_spec, y_spec, rows(D_MODEL), _full(gb.shape), _full(wout.shape), _full(cw.shape),
                 _full(cbias.shape)]
    args += [zstate, y_arr, x, gb, wout, cw, cbias]
    n_tiles = n // tm
    return pl.pallas_call(
        functools.partial(_mix_out_kernel, tm=tm, halo=halo, row_stride=row_stride, tiles_per_seq=tiles_per_seq),
        grid=(n_tiles,),
        in_specs=in_specs,
        out_specs=[rows(D_MODEL), pl.BlockSpec((pre, W_CONV), lambda i: (i, 0))],
        out_shape=[jax.ShapeDtypeStruct((n, D_MODEL), F32), jax.ShapeDtypeStruct((n_tiles * pre, W_CONV), F32)],
        scratch_shapes=[pltpu.VMEM((tm + pre, W_CONV), F32)],
        compiler_params=_cparams(("parallel",)),
    )(*args)


def _ffn_kernel(*refs, moe, final):
    refs = list(refs)
    x_ref, g_ref = refs[:2]
    pos = 2
    if moe:
        wr_ref, br_ref = refs[pos:pos + 2]
        pos += 2
    wg_ref, wu_ref, wd_ref = refs[pos:pos + 3]
    pos += 3
    if final:
        gf_ref = refs[pos]
        pos += 1
    o_ref, h_sc, acc_sc = refs[pos:pos + 3]
    pos += 3
    if moe:
        comb_sc = refs[pos]
    e = pl.program_id(1)
    c = pl.program_id(2)
    last = (e == pl.num_programs(1) - 1) & (c == pl.num_programs(2) - 1)

    @pl.when((e == 0) & (c == 0))
    def _():
        h = _rms(x_ref[...]) * g_ref[...]
        h_sc[...] = h.astype(BF16)
        acc_sc[...] = jnp.zeros_like(acc_sc)
        if moe:
            logits = _dot_split(h, wr_ref[...]) + br_ref[...]
            lane = lax.broadcasted_iota(jnp.int32, logits.shape, 1)
            logits = jnp.where(lane < N_EXPERTS, logits, -jnp.inf)
            keep = _top_k_mask(logits, 0, N_EXPERTS, EXPERT_TOP_K) & (lane < N_EXPERTS)
            top = jnp.max(logits, axis=1, keepdims=True)
            w = jnp.where(keep, jnp.exp(logits - top), 0.0)
            comb_sc[...] = w / jnp.sum(w, axis=1, keepdims=True)

    hb = h_sc[...]
    a = jnp.dot(hb, wg_ref[...], preferred_element_type=F32)
    b = jnp.dot(hb, wu_ref[...], preferred_element_type=F32)
    t = (a * jax.nn.sigmoid(a) * b).astype(BF16)
    d = jnp.dot(t, wd_ref[...], preferred_element_type=F32)
    if moe:
        d = d * _lane_column(comb_sc[...], e)
    acc_sc[...] += d

    @pl.when(last)
    def _():
        out = x_ref[...] + acc_sc[...]
        if final:
            out = _rms(out) * gf_ref[...]
        o_ref[...] = out


def _ffn(x, g, wg, wu, wd, tm, f_chunk, router=None, g_final=None):
    n = x.shape[0]
    n_exp, _, f_dim = wg.shape
    moe = router is not None
    final = g_final is not None
    rows = pl.BlockSpec((tm, D_MODEL), lambda i, e, c: (i, 0))
    const = lambda shape: pl.BlockSpec(shape, lambda i, e, c: (0,) * len(shape))
    in_specs = [rows, const((1, D_MODEL))]
    args = [x, g]
    if moe:
        in_specs += [const(router[0].shape), const(router[1].shape)]
        args += list(router)
    in_specs += [pl.BlockSpec((None, D_MODEL, f_chunk), lambda i, e, c: (e, 0, c)),
                 pl.BlockSpec((None, D_MODEL, f_chunk), lambda i, e, c: (e, 0, c)),
                 pl.BlockSpec((None, f_chunk, D_MODEL), lambda i, e, c: (e, c, 0))]
    args += [wg, wu, wd]
    if final:
        in_specs.append(const((1, D_MODEL)))
        args.append(g_final)
    scratch = [pltpu.VMEM((tm, D_MODEL), BF16), pltpu.VMEM((tm, D_MODEL), F32)]
    if moe:
        scratch.append(pltpu.VMEM((tm, LANES), F32))
    return pl.pallas_call(
        functools.partial(_ffn_kernel, moe=moe, final=final),
        grid=(n // tm, n_exp, f_dim // f_chunk),
        in_specs=in_specs,
        out_specs=rows,
        out_shape=jax.ShapeDtypeStruct((n, D_MODEL), F32),
        scratch_shapes=scratch,
        compiler_params=_cparams(("parallel", "arbitrary", "arbitrary")),
    )(*args)


MOE_TILE = 1024
MOE_CAP = 288


def _router_kernel(x_ref, g_ref, wr_ref, br_ref, h_ref, comb_ref, cnt_ref):
    h = _rms(x_ref[...]) * g_ref[...]
    h_ref[...] = h.astype(BF16)
    logits = _dot_split(h, wr_ref[...]) + br_ref[...]
    lane = lax.broadcasted_iota(jnp.int32, logits.shape, 1)
    logits = jnp.where(lane < N_EXPERTS, logits, -jnp.inf)
    keep = _top_k_mask(logits, 0, N_EXPERTS, EXPERT_TOP_K) & (lane < N_EXPERTS)
    w = jnp.where(keep, jnp.exp(logits - jnp.max(logits, axis=1, keepdims=True)), 0.0)
    comb = w / jnp.sum(w, axis=1, keepdims=True)
    comb_ref[...] = comb
    count = jnp.sum(jnp.where(comb > 0.0, 1.0, 0.0), axis=0, keepdims=True)
    cnt_ref[...] = jnp.broadcast_to(count, cnt_ref.shape)


def _router(x, g, wr, br, tm):
    n = x.shape[0]
    n_tiles = n // tm
    h, comb, cnt = pl.pallas_call(
        _router_kernel,
        grid=(n_tiles,),
        in_specs=[pl.BlockSpec((tm, D_MODEL), lambda i: (i, 0)), _full((1, D_MODEL)), _full(wr.shape), _full(br.shape)],
        out_specs=[pl.BlockSpec((tm, D_MODEL), lambda i: (i, 0)), pl.BlockSpec((tm, LANES), lambda i: (i, 0)),
                   pl.BlockSpec((SUBLANES, LANES), lambda i: (i, 0))],
        out_shape=[jax.ShapeDtypeStruct((n, D_MODEL), BF16), jax.ShapeDtypeStruct((n, LANES), F32),
                   jax.ShapeDtypeStruct((n_tiles * SUBLANES, LANES), F32)],
        compiler_params=_cparams(("parallel",)),
    )(x, g, wr, br)
    counts = cnt.reshape(n_tiles, SUBLANES, LANES)[:, 0, :N_EXPERTS].astype(jnp.int32).reshape(-1)
    return h, comb, counts


def _moe_kernel(cnt_ref, x_ref, h_ref, comb_ref, wg_ref, wu_ref, wd_ref, gf_ref, o_ref,
                tri_sc, rank_sc, maskt_sc, rankt_sc, *, tm, cap, final):
    i = pl.program_id(0)
    e = pl.program_id(1)
    cap_pad = -(-cap // LANES) * LANES
    sub = 256

    @pl.when((i == 0) & (e == 0))
    def _():
        for rc in range(tm // sub):
            r = lax.broadcasted_iota(jnp.int32, (sub, tm), 0) + rc * sub
            c = lax.broadcasted_iota(jnp.int32, (sub, tm), 1)
            tri_sc[rc * sub:(rc + 1) * sub, :] = jnp.where(c < r, 1.0, 0.0).astype(BF16)

    @pl.when(e == 0)
    def _():
        routed = jnp.where(comb_ref[...] > 0.0, 1.0, 0.0).astype(BF16)
        rank_sc[...] = jnp.dot(tri_sc[...], routed, preferred_element_type=F32)
        er = lax.broadcasted_iota(jnp.int32, (LANES, LANES), 0)
        ec = lax.broadcasted_iota(jnp.int32, (LANES, LANES), 1)
        eye = jnp.where(er == ec, 1.0, 0.0).astype(BF16)
        routed_t = lax.dot_general(eye, routed, NT_DIMS, preferred_element_type=F32)
        rank_t = lax.dot_general(routed_t.astype(BF16), tri_sc[...], NT_DIMS, preferred_element_type=F32)
        maskt_sc[...] = routed_t[0:SUBLANES]
        rankt_sc[...] = rank_t[0:SUBLANES]
        o_ref[...] = x_ref[...]

    count = cnt_ref[i * N_EXPERTS + e]
    gate_col = _lane_column(comb_ref[...], e)
    rank_col = _lane_column(rank_sc[...], e)
    routed_row = maskt_sc[pl.ds(e, 1), :]
    rank_row = rankt_sc[pl.ds(e, 1), :]
    for ch in range(-(-tm // cap)):
        @pl.when(count > ch * cap)
        def _(ch=ch):
            slot = lax.broadcasted_iota(jnp.int32, (cap, tm), 0).astype(F32) + float(ch * cap)
            pick = jnp.where((rank_row == slot) & (routed_row > 0.5), 1.0, 0.0).astype(BF16)
            xg = jnp.dot(pick, h_ref[...], preferred_element_type=F32).astype(BF16)
            a = jnp.dot(xg, wg_ref[...], preferred_element_type=F32)
            b = jnp.dot(xg, wu_ref[...], preferred_element_type=F32)
            t = (a * jax.nn.sigmoid(a) * b).astype(BF16)
            y = jnp.dot(t, wd_ref[...], preferred_element_type=F32).astype(BF16)
            if cap_pad > cap:
                y = jnp.concatenate([y, jnp.zeros((cap_pad - cap, D_MODEL), BF16)], axis=0)
            slot_l = lax.broadcasted_iota(jnp.int32, (tm, cap_pad), 1).astype(F32) + float(ch * cap)
            place = jnp.where((rank_col == slot_l) & (gate_col > 0.0), 1.0, 0.0).astype(BF16)
            o_ref[...] += gate_col * jnp.dot(place, y, preferred_element_type=F32)

    if final:
        @pl.when(e == pl.num_programs(1) - 1)
        def _():
            o_ref[...] = _rms(o_ref[...]) * gf_ref[...]


def _moe_sparse(x, g, wg, wu, wd, router, g_final=None):
    n = x.shape[0]
    tm, cap = MOE_TILE, MOE_CAP
    assert n % tm == 0 and cap % (2 * SUBLANES) == 0 and tm % 256 == 0
    n_exp, _, f_dim = wg.shape
    h, comb, counts = _router(x, g, router[0], router[1], tm)
    final = g_final is not None
    gf = g_final if final else jnp.ones((1, D_MODEL), F32)
    rows = lambda w: pl.BlockSpec((tm, w), lambda i, e, cnt: (i, 0))
    grid_spec = pltpu.PrefetchScalarGridSpec(
        num_scalar_prefetch=1,
        grid=(n // tm, n_exp),
        in_specs=[rows(D_MODEL), rows(D_MODEL), rows(LANES),
                  pl.BlockSpec((None, D_MODEL, f_dim), lambda i, e, cnt: (e, 0, 0)),
                  pl.BlockSpec((None, D_MODEL, f_dim), lambda i, e, cnt: (e, 0, 0)),
                  pl.BlockSpec((None, f_dim, D_MODEL), lambda i, e, cnt: (e, 0, 0)),
                  pl.BlockSpec((1, D_MODEL), lambda i, e, cnt: (0, 0))],
        out_specs=rows(D_MODEL),
        scratch_shapes=[pltpu.VMEM((tm, tm), BF16),
                        pltpu.VMEM((tm, LANES), F32),
                        pltpu.VMEM((SUBLANES, tm), F32),
                        pltpu.VMEM((SUBLANES, tm), F32)],
    )
    return pl.pallas_call(
        functools.partial(_moe_kernel, tm=tm, cap=cap, final=final),
        grid_spec=grid_spec,
        out_shape=jax.ShapeDtypeStruct((n, D_MODEL), F32),
        compiler_params=_cparams(("arbitrary", "arbitrary")),
    )(counts, x, h, comb, wg, wu, wd, gf)


def _alibi_slopes_np():
    return (2.0 ** (-8.0 * np.arange(1, N_HEADS + 1) / N_HEADS)).astype(np.float32)


def _ssm_params(a_re, a_im, log_dt, b_re, b_im, c_re, c_im):
    ar, ai = a_re.astype(F32), a_im.astype(F32)
    dt = jnp.exp(log_dt.astype(F32))[:, None]
    mag = jnp.exp(dt * ar)
    abr, abi = mag * jnp.cos(dt * ai), mag * jnp.sin(dt * ai)
    den = ar * ar + ai * ai
    zr = ((abr - 1.0) * ar + abi * ai) / den
    zi = (abi * ar - (abr - 1.0) * ai) / den
    br, bim = b_re.astype(F32), b_im.astype(F32)
    bbr = zr[..., None] * br - zi[..., None] * bim
    bbi = zr[..., None] * bim + zi[..., None] * br
    eye = jnp.eye(N_GROUPS, dtype=F32)
    to_in = lambda m: jnp.einsum('gnc,gh->gchn', m, eye).reshape(W_SSM, N_STATE)
    to_out = lambda m: jnp.einsum('gcn,gh->gnhc', m, eye).reshape(N_STATE, W_SSM)
    bbd = jnp.concatenate([to_in(bbr), to_in(bbi)], axis=1).astype(BF16)
    cbd = jnp.concatenate([to_out(c_re.astype(F32)), -to_out(c_im.astype(F32))], axis=0).astype(BF16)
    a = jnp.stack([abr.reshape(N_STATE), abi.reshape(N_STATE)])
    return a, bbd, cbd


def _trunk(x, w, depth, bsz, seq, time_major, attend, conv0, h0r, h0i):
    n = bsz * seq
    ks, vs, zs, hrs, his = [], [], [], [], []
    if time_major:
        tm, tiles_per_seq, row_stride, halo = n, 1, bsz, False
        pre = (CONV_WIDTH - 1) * bsz
        u_shape = (n, W_SSM)
        u_spec = pl.BlockSpec((tm, W_SSM), lambda i: (0, 0))
        kv_shape = (n, W_ATT)
        kv_spec = pl.BlockSpec((tm, W_ATT), lambda i: (0, 0))
        zstate_spec = pl.BlockSpec((pre, W_CONV), lambda i: (0, 0))
        t_chunk = seq
    else:
        tm, row_stride, halo = ROW_TILE, 1, True
        tiles_per_seq = seq // tm
        pre = SUBLANES
        u_shape = (seq, bsz * W_SSM)
        u_spec = pl.BlockSpec((tm, W_SSM), lambda i: (i % tiles_per_seq, i // tiles_per_seq))
        kv_shape = (depth, bsz, W_ATT, seq)
        zstate_spec = pl.BlockSpec((None, pre, W_CONV), lambda i: (i // tiles_per_seq, 0, 0))
        t_chunk = 32
    k = v = None
    for l in range(depth):
        if not time_major:
            kv_spec = pl.BlockSpec((None, None, W_ATT, tm),
                                   lambda i, l=l: (l, i // tiles_per_seq, 0, i % tiles_per_seq))
        q, k, v, cb, cc, cx, u = _mix_in(x, w['g_mix'][l], w['w_in'][l], w['w_kvt'][l], tm, u_shape, u_spec,
                                         kv_shape, kv_spec, kv_prev=None if time_major or l == 0 else (k, v))
        att = attend(l, q, k, v)
        y, hr, hi = _ssm(u, h0r[l], h0i[l], *w['ssm'][l], bsz, seq, t_chunk)
        x, zlast = _mix_out(att, cb, cc, cx, conv0[l], y, u_spec, x, w['g_branch'][l], w['w_out'][l],
                            w['conv_w'][l], w['conv_b'][l], tm, halo, row_stride, tiles_per_seq, zstate_spec)
        g_final = w['g_final'] if l == depth - 1 else None
        if l % 2 == 0:
            x = _ffn(x, w['g_ffn'][l], *w['dense'][l // 2], tm, w['dense'][l // 2][0].shape[-1] // 2, g_final=g_final)
        else:
            wg, wu, wd, router = w['moe'][l // 2]
            if x.shape[0] % MOE_TILE == 0:
                x = _moe_sparse(x, w['g_ffn'][l], wg, wu, wd, router, g_final=g_final)
            else:
                x = _ffn(x, w['g_ffn'][l], wg, wu, wd, tm, wg.shape[-1], router=router, g_final=g_final)
        ks.append(k)
        vs.append(v)
        zs.append(zlast)
        hrs.append(hr)
        his.append(hi)
    return x, ks, vs, zs, hrs, his


def kernel(x_prompt, x_sample, cache_k, cache_v, state_conv, state_ssm_re, state_ssm_im, page_table, g_mix_norm, w_in, g_branch, w_out, conv_w, conv_b, ssm_a_re, ssm_a_im, ssm_log_dt, ssm_b_re, ssm_b_im, ssm_c_re, ssm_c_im, ssm_d, ssm_w_glu, g_ffn_norm, w_ffn_gate, w_ffn_up, w_ffn_down, w_router, b_router, w_exp_gate, w_exp_up, w_exp_down, g_final):
    depth = w_in.shape[0]
    bp, sp, _ = x_prompt.shape
    bs, ss, _ = x_sample.shape
    n_pool, page = cache_k.shape[1], cache_k.shape[2]
    slopes_np = _alibi_slopes_np()
    slopes = jnp.asarray(slopes_np)

    row = lambda a: a.reshape(a.shape[0], 1, a.shape[-1])
    w_in_bf16 = w_in.astype(BF16)
    w = {
        'g_mix': row(g_mix_norm), 'w_in': w_in_bf16, 'g_branch': row(g_branch), 'w_out': w_out.astype(BF16),
        'w_kvt': jnp.swapaxes(w_in_bf16[:, :, W_ATT:3 * W_ATT], 1, 2),
        'conv_w': conv_w, 'conv_b': row(conv_b), 'g_ffn': row(g_ffn_norm), 'g_final': g_final.reshape(1, D_MODEL),
        'ssm': [(*_ssm_params(ssm_a_re[l], ssm_a_im[l], ssm_log_dt[l], ssm_b_re[l], ssm_b_im[l], ssm_c_re[l],
                              ssm_c_im[l]), ssm_d[l].reshape(1, W_SSM), ssm_w_glu[l].astype(BF16))
                for l in range(depth)],
        'dense': [(w_ffn_gate[j:j + 1].astype(BF16), w_ffn_up[j:j + 1].astype(BF16), w_ffn_down[j:j + 1].astype(BF16))
                  for j in range(w_ffn_gate.shape[0])],
        'moe': [(w_exp_gate[j].astype(BF16), w_exp_up[j].astype(BF16), w_exp_down[j].astype(BF16),
                 (jnp.pad(w_router[j], ((0, 0), (0, LANES - N_EXPERTS))),
                  jnp.pad(b_router[j].reshape(1, N_EXPERTS), ((0, 0), (0, LANES - N_EXPERTS)))))
                for j in range(w_exp_gate.shape[0])],
    }

    conv0_p = jnp.zeros((depth, bp, SUBLANES, W_CONV), F32)
    h0_p = jnp.zeros((depth, bp, N_STATE), F32)
    attend_p = lambda l, q, kt, vt: _moba_prompt(q, kt, vt, l, slopes, bp, sp)
    yp, kts, vts, zs, hrs, his = _trunk(x_prompt.reshape(bp * sp, D_MODEL), w, depth, bp, sp, False, attend_p,
                                        conv0_p, h0_p, h0_p)
    y_prompt = yp.reshape(bp, sp, D_MODEL)
    from_t = lambda a: jnp.transpose(a[-1].reshape(depth, bp, N_HEADS, HEAD_DIM, sp), (0, 1, 4, 2, 3))
    k_p, v_p = from_t(kts), from_t(vts)
    tiles = sp // ROW_TILE
    conv_p = jnp.stack(zs).reshape(depth, bp, tiles, SUBLANES, W_CONV)[:, :, -1, SUBLANES - (CONV_WIDTH - 1):, :]
    sre_p = jnp.stack(hrs).reshape(depth, bp, N_GROUPS, SSM_STATE)
    sim_p = jnp.stack(his).reshape(depth, bp, N_GROUPS, SSM_STATE)

    to_t = lambda c: jnp.transpose(c, (0, 1, 3, 4, 2)).reshape(depth, n_pool, W_ATT, page)
    cache_kt, cache_vt = to_t(cache_k), to_t(cache_v)
    xs = jnp.swapaxes(x_sample, 0, 1).reshape(ss * bs, D_MODEL)
    conv0_s = jnp.swapaxes(state_conv, 1, 2).reshape(depth, (CONV_WIDTH - 1) * bs, W_CONV)
    h0r_s = state_ssm_re.reshape(depth, bs, N_STATE)
    h0i_s = state_ssm_im.reshape(depth, bs, N_STATE)
    attend_s = lambda l, q, k, v: _moba_sample(q, k, v, cache_kt, cache_vt, l, page_table, slopes_np, bs, ss)
    ys, ks, vs, zs, hrs, his = _trunk(xs, w, depth, bs, ss, True, attend_s, conv0_s, h0r_s, h0i_s)
    from_tb = lambda a, width: jnp.swapaxes(a.reshape(-1, ss, bs, width), 1, 2)
    y_sample = from_tb(ys, D_MODEL)[0]
    k_s = from_tb(jnp.stack(ks), W_ATT).reshape(depth, bs, ss, N_HEADS, HEAD_DIM)
    v_s = from_tb(jnp.stack(vs), W_ATT).reshape(depth, bs, ss, N_HEADS, HEAD_DIM)
    conv_s = jnp.swapaxes(jnp.stack(zs).reshape(depth, CONV_WIDTH - 1, bs, W_CONV), 1, 2)
    sre_s = jnp.stack(hrs).reshape(depth, bs, N_GROUPS, SSM_STATE)
    sim_s = jnp.stack(his).reshape(depth, bs, N_GROUPS, SSM_STATE)
    return (y_prompt, y_sample, k_p, v_p, conv_p, sre_p, sim_p, k_s, v_s, conv_s, sre_s, sim_s)
```

```python
import functools

import numpy as np
import jax
import jax.numpy as jnp
from jax import lax
from jax.experimental import pallas as pl
from jax.experimental.pallas import tpu as pltpu

F32 = jnp.float32
BF16 = jnp.bfloat16
HIGHEST = lax.Precision.HIGHEST

D_MODEL = 1024
N_HEADS = 8
HEAD_DIM = 64
W_ATT = N_HEADS * HEAD_DIM
W_CONV = 256
W_SSM = 256
CONV_WIDTH = 3
SSM_GROUP = 16
N_GROUPS = 16
SSM_STATE = 64
N_STATE = N_GROUPS * SSM_STATE
MOBA_BLOCK = 256
MOBA_TOP_K = 3
ATT_SCALE = HEAD_DIM ** -0.5
N_EXPERTS = 8
EXPERT_TOP_K = 2
RMS_EPS = 1e-6
NEG = -1e30

LANES = 128
SUBLANES = 8
VMEM_LIMIT = 56 * 1024 * 1024
ROW_TILE = 512
NT_DIMS = (((1,), (1,)), ((), ()))


def _cparams(sem):
    return pltpu.CompilerParams(dimension_semantics=sem, vmem_limit_bytes=VMEM_LIMIT)


def _rms(x):
    return x * lax.rsqrt(jnp.mean(x * x, axis=-1, keepdims=True) + RMS_EPS)


def _full(shape):
    n = len(shape)
    return pl.BlockSpec(shape, lambda *_: (0,) * n)


def _split_bf16(x):
    hi = x.astype(BF16)
    return hi, (x - hi.astype(F32)).astype(BF16)


def _dot_split(a, b):
    a_hi, a_lo = _split_bf16(a)
    b_hi, b_lo = _split_bf16(b)
    dot = functools.partial(jnp.dot, preferred_element_type=F32)
    return dot(a_hi, b_hi) + (dot(a_hi, b_lo) + dot(a_lo, b_hi))


def _lane_column(x, idx):
    lane = lax.broadcasted_iota(jnp.int32, x.shape, 1)
    return jnp.sum(jnp.where(lane == idx, x, 0.0), axis=1, keepdims=True)


def _top_k_mask(score, first, n_cand, k):
    lane = lax.broadcasted_iota(jnp.int32, score.shape, 1)
    rank = jnp.zeros(score.shape, jnp.int32)
    for m in range(first, first + n_cand):
        sm = score[:, m:m + 1]
        rank += ((sm > score) | ((sm == score) & (m < lane))).astype(jnp.int32)
    return rank < k


def _mix_in_kernel(x_ref, g_ref, w_ref, wkvt_ref, *rest, kv_transposed):
    q_ref, k_ref, v_ref, cb_ref, cc_ref, cx_ref, u_ref = rest[-7:]
    h = (_rms(x_ref[...]) * g_ref[...]).astype(BF16)
    outs = (q_ref, k_ref, v_ref, cb_ref, cc_ref, cx_ref, u_ref)
    widths = (W_ATT,) * 3 + (W_CONV,) * 3 + (W_SSM,)
    col = 0
    for idx, (ref, width) in enumerate(zip(outs, widths)):
        if kv_transposed and idx in (1, 2):
            ref[...] = lax.dot_general(wkvt_ref[(idx - 1) * W_ATT:idx * W_ATT, :], h, NT_DIMS,
                                       preferred_element_type=F32)
        else:
            ref[...] = jnp.dot(h, w_ref[:, col:col + width], preferred_element_type=F32)
        col += width


def _mix_in(x, g, w_bf16, wkvt_bf16, tm, u_shape, u_spec, kv_shape, kv_spec, kv_prev=None):
    n = x.shape[0]
    rows = lambda w: pl.BlockSpec((tm, w), lambda i: (i, 0))
    conv_widths = (W_CONV, W_CONV, W_CONV)
    in_specs = [rows(D_MODEL), _full((1, D_MODEL)), _full(w_bf16.shape), _full(wkvt_bf16.shape)]
    args = [x, g, w_bf16, wkvt_bf16]
    aliases = {}
    if kv_prev is not None:
        in_specs += [pl.BlockSpec(memory_space=pl.ANY)] * 2
        args += list(kv_prev)
        aliases = {4: 1, 5: 2}
    return pl.pallas_call(
        functools.partial(_mix_in_kernel, kv_transposed=len(kv_shape) == 4),
        grid=(n // tm,),
        in_specs=in_specs,
        out_specs=[rows(W_ATT), kv_spec, kv_spec] + [rows(w) for w in conv_widths] + [u_spec],
        out_shape=[jax.ShapeDtypeStruct((n, W_ATT), F32), jax.ShapeDtypeStruct(kv_shape, F32),
                   jax.ShapeDtypeStruct(kv_shape, F32)]
                  + [jax.ShapeDtypeStruct((n, w), F32) for w in conv_widths] + [jax.ShapeDtypeStruct(u_shape, F32)],
        input_output_aliases=aliases,
        compiler_params=_cparams(("parallel",)),
    )(*args)


AUG_KPOS = 32


def _moba_prompt_consts(seq):
    n_blocks = seq // MOBA_BLOCK
    kc = np.zeros((2, 2 * HEAD_DIM, seq), np.float32)
    vc = np.zeros((2, 2 * HEAD_DIM, seq), np.float32)
    for hl in range(2):
        aug = HEAD_DIM * (1 - hl)
        for n in range(n_blocks):
            kc[hl, aug + n, n * MOBA_BLOCK:(n + 1) * MOBA_BLOCK] = 1.0
        kc[hl, aug + AUG_KPOS, :] = np.arange(seq) % MOBA_BLOCK
        vc[hl, aug, :] = 1.0
    return jnp.asarray(kc), jnp.asarray(vc)


def _moba_prompt_kernel(slopes_ref, q_ref, kt_ref, vt_ref, kc_ref, vc_ref, o_ref, kb_sc, vb_sc, *, n_blocks):
    j = pl.program_id(1)
    blk = MOBA_BLOCK
    pair = 2 * HEAD_DIM
    kt = kt_ref[...]
    vt = vt_ref[...]
    row = lax.broadcasted_iota(jnp.int32, kt.shape, 0)
    for hl in range(2):
        own_rows = (row >= HEAD_DIM * hl) & (row < HEAD_DIM * (hl + 1))
        kb_sc[hl] = jnp.where(own_rows, kt, kc_ref[hl]).astype(BF16)
        vb_sc[hl] = jnp.where(own_rows, vt, vc_ref[hl]).astype(BF16)

    krow = lax.broadcasted_iota(jnp.int32, (pair, LANES), 0)
    klane = lax.broadcasted_iota(jnp.int32, (pair, LANES), 1)
    aug_of_row = jnp.where(krow < HEAD_DIM, HEAD_DIM, 0)
    kmean = jnp.zeros((pair, LANES), F32)
    for n in range(n_blocks):
        mean_n = jnp.mean(kt[:, n * blk:(n + 1) * blk], axis=1, keepdims=True)
        kmean = jnp.where(klane == aug_of_row + n, mean_n, kmean)
    kmean_hi, kmean_lo = _split_bf16(kmean)

    lane = lax.broadcasted_iota(jnp.int32, (blk, pair), 1)
    r_io = lax.broadcasted_iota(jnp.int32, (blk, blk), 0)
    c_io = lax.broadcasted_iota(jnp.int32, (blk, blk), 1)
    causal = c_io <= r_io
    for i in range(n_blocks):
        q_i = q_ref[i * blk:(i + 1) * blk, :]
        past, width = i * blk, (i + 1) * blk
        if i > MOBA_TOP_K:
            q_hi, q_lo = _split_bf16(q_i)
            dot = functools.partial(jnp.dot, preferred_element_type=F32)
            gate = dot(q_hi, kmean_hi) + (dot(q_hi, kmean_lo) + dot(q_lo, kmean_hi))
        outs = []
        for hl in range(2):
            aug = HEAD_DIM * (1 - hl)
            slope = slopes_ref[2 * j + hl]
            qh = jnp.where((lane >= HEAD_DIM * hl) & (lane < HEAD_DIM * (hl + 1)), q_i, 0.0)
            is_past = (lane >= aug) & (lane < aug + i)
            if i > MOBA_TOP_K:
                keep =_top_k_mask(jnp.where(is_past, gate, -jnp.inf), aug, i, MOBA_TOP_K) & is_past
            else:
                keep = is_past
            bias = jnp.where(keep | (lane == aug + i), 0.0, NEG) + slope * (blk * (lane - (aug + i)).astype(F32))
            q_aug = jnp.where((lane >= aug) & (lane <= aug + i), bias, qh * ATT_SCALE)
            q_aug = jnp.where(lane == aug + AUG_KPOS, slope, q_aug).astype(BF16)

            s = jnp.dot(q_aug, kb_sc[hl, :, 0:width], preferred_element_type=F32)
            s_own = jnp.where(causal, s[:, past:width], NEG)
            m = jnp.max(s_own, axis=1, keepdims=True)
            if i > 0:
                m = jnp.maximum(m, jnp.max(s[:, 0:past], axis=1, keepdims=True))
            o = lax.dot_general(jnp.exp(s_own - m).astype(BF16), vb_sc[hl, :, past:width], NT_DIMS,
                                preferred_element_type=F32)
            if i > 0:
                o = o + lax.dot_general(jnp.exp(s[:, 0:past] - m).astype(BF16), vb_sc[hl, :, 0:past], NT_DIMS,
                                        preferred_element_type=F32)
            outs.append(o * (1.0 / o[:, aug:aug + 1]))
        o_ref[i * blk:(i + 1) * blk, :] = jnp.where(lane < HEAD_DIM, outs[0], outs[1])


def _moba_prompt(q, kt, vt, layer, slopes, bsz, seq):
    assert seq % MOBA_BLOCK == 0
    n_blocks = seq // MOBA_BLOCK
    assert n_blocks <= AUG_KPOS < HEAD_DIM
    pair = 2 * HEAD_DIM
    kc, vc = _moba_prompt_consts(seq)
    kv_spec = pl.BlockSpec((None, None, pair, seq), lambda b, j: (layer, b, j, 0))
    qo_spec = pl.BlockSpec((None, seq, pair), lambda b, j: (b, 0, j))
    const_spec = pl.BlockSpec((2, pair, seq), lambda b, j: (0, 0, 0))
    out = pl.pallas_call(
        functools.partial(_moba_prompt_kernel, n_blocks=n_blocks),
        grid=(bsz, W_ATT // pair),
        in_specs=[pl.BlockSpec(memory_space=pltpu.SMEM), qo_spec, kv_spec, kv_spec, const_spec, const_spec],
        out_specs=qo_spec,
        out_shape=jax.ShapeDtypeStruct((bsz, seq, W_ATT), F32),
        scratch_shapes=[pltpu.VMEM((2, pair, seq), BF16), pltpu.VMEM((2, pair, seq), BF16)],
        compiler_params=_cparams(("parallel", "parallel")),
    )(slopes, q.reshape(bsz, seq, W_ATT), kt, vt, kc, vc)
    return out.reshape(bsz * seq, W_ATT)


PAGES_PER_STEP = 32


def _moba_sample_kernel(pt_ref, slope_ref, tq_ref, q_ref, kn_ref, vn_ref, *rest, n_pages, page, t_new):
    del pt_ref
    k_refs = rest[:PAGES_PER_STEP]
    v_refs = rest[PAGES_PER_STEP:2 * PAGES_PER_STEP]
    o_ref, st_sc, selb_sc, ksum_sc, qbd_sc, qbdf_sc, own_sc, inv_sc, acc_sc = rest[2 * PAGES_PER_STEP:]
    ph = pl.program_id(1)
    pg = pl.program_id(2)
    n_groups = n_pages // PAGES_PER_STEP
    ppb = MOBA_BLOCK // page
    n_blocks = n_pages // ppb
    past = n_pages * page
    ht = N_HEADS * t_new

    @pl.when((ph == 0) & (pg == 0))
    def _():
        qt = jnp.concatenate([q_ref[...]] * N_HEADS, axis=0)
        row = lax.broadcasted_iota(jnp.int32, (ht, W_ATT), 0)
        col = lax.broadcasted_iota(jnp.int32, (ht, W_ATT), 1)
        qbd = jnp.where(col // HEAD_DIM == row // t_new, qt, 0.0) * ATT_SCALE
        qbdf_sc[...] = qbd
        qbd_sc[...] = qbd.astype(BF16)
        ksum_sc[...] = jnp.zeros_like(ksum_sc)

    @pl.when(ph == 0)
    def _():
        lane = lax.broadcasted_iota(jnp.int32, (W_ATT, LANES), 1)
        for bb in range(PAGES_PER_STEP // ppb):
            tot = None
            for pp in range(ppb):
                ii = bb * ppb + pp
                kp = k_refs[ii][...]
                tot = kp if tot is None else tot + kp
                st_sc[pg * PAGES_PER_STEP + ii] = jnp.dot(qbd_sc[...], kp.astype(BF16), preferred_element_type=F32)
            blk_idx = pg * (PAGES_PER_STEP // ppb) + bb
            ksum_sc[...] = jnp.where(lane == blk_idx, jnp.sum(tot, axis=1, keepdims=True), ksum_sc[...])

    @pl.when((ph == 0) & (pg == n_groups - 1))
    def _():
        slope = slope_ref[...]
        tq = tq_ref[...]
        lane = lax.broadcasted_iota(jnp.int32, (ht, LANES), 1)
        lane_f = lane.astype(F32)
        gate = jnp.dot(qbdf_sc[...], ksum_sc[...], precision=HIGHEST, preferred_element_type=F32)
        gate = jnp.where(lane < n_blocks, gate, -jnp.inf)
        keep = _top_k_mask(gate, 0, n_blocks, min(MOBA_TOP_K, n_blocks)) & (lane < n_blocks)
        sel_bias = jnp.where(keep, 0.0, NEG)
        for n in range(n_blocks):
            selb_sc[n] = jnp.broadcast_to(sel_bias[:, n:n + 1], (ht, LANES))

        k_own = jnp.concatenate([kn_ref[...], jnp.zeros((LANES - t_new, W_ATT), F32)], axis=0)
        so = lax.dot_general(qbd_sc[...], k_own.astype(BF16), NT_DIMS, preferred_element_type=F32)
        lo = jnp.where((lane_f <= tq) & (lane < t_new), so - slope * (tq - lane_f), NEG)
        qpos = tq + float(past)

        def logits(n, pp):
            pidx = n * ppb + pp
            dist = qpos - (lane_f + jnp.asarray(pidx * page, F32))
            return pidx, st_sc[pidx] - slope * dist + selb_sc[n]

        def max_body(n, m_vec):
            for pp in range(ppb):
                m_vec = jnp.maximum(m_vec, logits(n, pp)[1])
            return m_vec

        m = jnp.max(lax.fori_loop(0, n_blocks, max_body, lo, unroll=4), axis=1, keepdims=True)
        eo = jnp.exp(lo - m)

        def exp_body(n, l_vec):
            for pp in range(ppb):
                pidx, lg = logits(n, pp)
                e = jnp.exp(lg - m)
                st_sc[pidx] = e
                l_vec = l_vec + e
            return l_vec

        l_vec = lax.fori_loop(0, n_blocks, exp_body, eo, unroll=4)
        inv = 1.0 / jnp.sum(l_vec, axis=1, keepdims=True)
        inv_sc[...] = jnp.broadcast_to(inv, inv_sc.shape)
        own_sc[...] = eo * inv
        acc_sc[...] = jnp.zeros_like(acc_sc)

    @pl.when(ph == 1)
    def _():
        inv = inv_sc[...]
        tot = None
        for ii in range(PAGES_PER_STEP):
            p = (st_sc[pg * PAGES_PER_STEP + ii] * inv).astype(BF16)
            d = lax.dot_general(p, v_refs[ii][...].astype(BF16), NT_DIMS, preferred_element_type=F32)
            tot = d if tot is None else tot + d
        acc_sc[...] += tot

    @pl.when((ph == 1) & (pg == n_groups - 1))
    def _():
        v_own = jnp.concatenate([vn_ref[...], jnp.zeros((LANES - t_new, W_ATT), F32)], axis=0)
        acc = acc_sc[...] + jnp.dot(own_sc[...].astype(BF16), v_own.astype(BF16), preferred_element_type=F32)
        col = lax.broadcasted_iota(jnp.int32, (t_new, W_ATT), 1)
        out = jnp.zeros((t_new, W_ATT), F32)
        for h in range(N_HEADS):
            out += jnp.where(col // HEAD_DIM == h, acc[h * t_new:(h + 1) * t_new, :], 0.0)
        o_ref[...] = out


def _moba_sample(q, k, v, cache_kt, cache_vt, layer, page_table, slopes_np, bsz, t_new):
    page = cache_kt.shape[-1]
    n_pages = page_table.shape[1]
    assert MOBA_BLOCK % page == 0 and (n_pages * page) % MOBA_BLOCK == 0 and n_pages % PAGES_PER_STEP == 0
    assert t_new % SUBLANES == 0 and t_new <= LANES and page == LANES
    ht = N_HEADS * t_new
    n_groups = n_pages // PAGES_PER_STEP
    n_blocks = n_pages * page // MOBA_BLOCK
    assert n_blocks <= LANES
    row_h = np.arange(ht) // t_new
    slope_r = jnp.asarray(np.broadcast_to(slopes_np[row_h][:, None], (ht, LANES)), F32)
    tq_r = jnp.asarray(np.broadcast_to((np.arange(ht) % t_new)[:, None], (ht, LANES)), F32)
    wide = lambda a: a.reshape(t_new, bsz * W_ATT)
    tok_spec = pl.BlockSpec((t_new, W_ATT), lambda b, ph, pg, pt: (0, b))
    const_spec = pl.BlockSpec((ht, LANES), lambda b, ph, pg, pt: (0, 0))

    def k_map(ii):
        def f(b, ph, pg, pt):
            p = jnp.where(ph == 0, pg, n_groups - 1) * PAGES_PER_STEP + ii
            return (layer, pt[b * n_pages + p], 0, 0)
        return f

    def v_map(ii):
        def f(b, ph, pg, pt):
            p = jnp.where(ph == 1, pg, 0) * PAGES_PER_STEP + ii
            return (layer, pt[b * n_pages + p], 0, 0)
        return f

    page_block = (None, None, W_ATT, page)
    grid_spec = pltpu.PrefetchScalarGridSpec(
        num_scalar_prefetch=1,
        grid=(bsz, 2, n_groups),
        in_specs=[const_spec, const_spec, tok_spec, tok_spec, tok_spec]
                 + [pl.BlockSpec(page_block, k_map(ii)) for ii in range(PAGES_PER_STEP)]
                 + [pl.BlockSpec(page_block, v_map(ii)) for ii in range(PAGES_PER_STEP)],
        out_specs=tok_spec,
        scratch_shapes=[pltpu.VMEM((n_pages, ht, page), F32),
                        pltpu.VMEM((n_blocks, ht, LANES), F32),
                        pltpu.VMEM((W_ATT, LANES), F32),
                        pltpu.VMEM((ht, W_ATT), BF16),
                        pltpu.VMEM((ht, W_ATT), F32),
                        pltpu.VMEM((ht, LANES), F32),
                        pltpu.VMEM((ht, LANES), F32),
                        pltpu.VMEM((ht, W_ATT), F32)],
    )
    out = pl.pallas_call(
        functools.partial(_moba_sample_kernel, n_pages=n_pages, page=page, t_new=t_new),
        grid_spec=grid_spec,
        out_shape=jax.ShapeDtypeStruct((t_new, bsz * W_ATT), F32),
        compiler_params=_cparams(("parallel", "arbitrary", "arbitrary")),
    )(page_table.reshape(-1), slope_r, tq_r, wide(q), wide(k), wide(v), *([cache_kt] * PAGES_PER_STEP),
      *([cache_vt] * PAGES_PER_STEP))
    return out.reshape(t_new * bsz, W_ATT)


SCAN_LANES = 256


def _gelu_tanh(x):
    return 0.5 * x * (1.0 + jnp.tanh(np.float32(np.sqrt(2.0 / np.pi)) * (x + 0.044715 * (x * x * x))))


def _ssm_kernel(u_ref, h0r_ref, h0i_ref, a_ref, bbd_ref, cbd_ref, d_ref, wglu_ref,
                y_ref, hr_ref, hi_ref, xs_sc, hr_sc, hi_sc, *wide_sc, bsz, t_chunk):
    step = pl.program_id(0)

    @pl.when(step == 0)
    def _():
        hr_sc[...] = h0r_ref[...]
        hi_sc[...] = h0i_ref[...]

    n_half = W_SSM // LANES
    if wide_sc:
        u_scs, y_scs = wide_sc[:n_half], wide_sc[n_half:]
        for b in range(bsz):
            for hf in range(n_half):
                col = b * W_SSM + hf * LANES
                u_scs[hf][pl.ds(b, t_chunk, stride=bsz), :] = u_ref[:, col:col + LANES]
        u = jnp.concatenate([sc[...] for sc in u_scs], axis=1)
    else:
        u = u_ref[...]
    xs_sc[...] = jnp.dot(u.astype(BF16), bbd_ref[...], preferred_element_type=F32)
    for lc in range(N_STATE // SCAN_LANES):
        re = slice(lc * SCAN_LANES, (lc + 1) * SCAN_LANES)
        im = slice(N_STATE + lc * SCAN_LANES, N_STATE + (lc + 1) * SCAN_LANES)
        ar = jnp.broadcast_to(a_ref[0:1, re], (bsz, SCAN_LANES))
        ai = jnp.broadcast_to(a_ref[1:2, re], (bsz, SCAN_LANES))

        def scan_step(t, carry, re=re, im=im, ar=ar, ai=ai):
            hr, hi = carry
            rows = pl.ds(pl.multiple_of(t * bsz, bsz), bsz)
            nr = ar * hr - ai * hi + xs_sc[rows, re]
            ni = ar * hi + ai * hr + xs_sc[rows, im]
            xs_sc[rows, re] = nr
            xs_sc[rows, im] = ni
            return nr, ni

        hr, hi = lax.fori_loop(0, t_chunk, scan_step, (hr_sc[:, re], hi_sc[:, re]))
        hr_sc[:, re] = hr
        hi_sc[:, re] = hi

    y = jnp.dot(xs_sc[...].astype(BF16), cbd_ref[...], preferred_element_type=F32) + d_ref[...] * u
    y = _gelu_tanh(y)
    y = y * jax.nn.sigmoid(jnp.dot(y.astype(BF16), wglu_ref[...], preferred_element_type=F32))
    if wide_sc:
        for hf in range(n_half):
            y_scs[hf][...] = y[:, hf * LANES:(hf + 1) * LANES]
        for b in range(bsz):
            for hf in range(n_half):
                col = b * W_SSM + hf * LANES
                y_ref[:, col:col + LANES] = y_scs[hf][pl.ds(b, t_chunk, stride=bsz), :]
    else:
        y_ref[...] = y

    @pl.when(step == pl.num_programs(0) - 1)
    def _():
        hr_ref[...] = hr_sc[...]
        hi_ref[...] = hi_sc[...]


def _ssm(u, h0r, h0i, a, bbd, cbd, d, wglu, bsz, seq, t_chunk):
    assert bsz % SUBLANES == 0 and seq % t_chunk == 0 and u.shape[0] in (seq, seq * bsz)
    rows = t_chunk * bsz
    wide = u.shape[0] == seq
    blk = pl.BlockSpec((t_chunk, bsz * W_SSM) if wide else (rows, W_SSM), lambda s: (s, 0))
    st_spec = _full((bsz, N_STATE))
    scratch = [pltpu.VMEM((rows, 2 * N_STATE), F32), pltpu.VMEM((bsz, N_STATE), F32), pltpu.VMEM((bsz, N_STATE), F32)]
    if wide:
        scratch += [pltpu.VMEM((rows, LANES), F32)] * (2 * (W_SSM // LANES))
    return pl.pallas_call(
        functools.partial(_ssm_kernel, bsz=bsz, t_chunk=t_chunk),
        grid=(seq // t_chunk,),
        in_specs=[blk, st_spec, st_spec, _full(a.shape), _full(bbd.shape), _full(cbd.shape), _full(d.shape),
                  _full(wglu.shape)],
        out_specs=[blk, st_spec, st_spec],
        out_shape=[jax.ShapeDtypeStruct(u.shape, F32), jax.ShapeDtypeStruct((bsz, N_STATE), F32),
                   jax.ShapeDtypeStruct((bsz, N_STATE), F32)],
        scratch_shapes=scratch,
        compiler_params=_cparams(("arbitrary",)),
    )(u, h0r, h0i, a, bbd, cbd, d, wglu)


def _mix_out_kernel(*refs, tm, halo, row_stride, tiles_per_seq):
    if halo:
        (att_ref, cb_ref, cc_ref, cx_ref, cch_ref, cxh_ref, zst_ref, y_ref, x_ref, gb_ref, wout_ref, cw_ref,
         cbias_ref, xo_ref, zlast_ref, zs_sc) = refs
    else:
        (att_ref, cb_ref, cc_ref, cx_ref, zst_ref, y_ref, x_ref, gb_ref, wout_ref, cw_ref,
         cbias_ref, xo_ref, zlast_ref, zs_sc) = refs
    pre = zs_sc.shape[0] - tm
    z = cc_ref[...] * cx_ref[...]
    if halo:
        first = pl.program_id(0) % tiles_per_seq == 0
        zs_sc[0:pre, :] = jnp.where(first, zst_ref[...], cch_ref[...] * cxh_ref[...])
    else:
        zs_sc[0:pre, :] = zst_ref[...]
    zs_sc[pre:, :] = z
    yc = cbias_ref[...]
    for tap in range(CONV_WIDTH - 1):
        off = pre - (CONV_WIDTH - 1 - tap) * row_stride
        yc = yc + cw_ref[tap:tap + 1, :] * zs_sc[off:off + tm, :]
    yc = yc + cw_ref[CONV_WIDTH - 1:CONV_WIDTH, :] * z
    conv_out = cb_ref[...] * yc
    zlast_ref[...] = zs_sc[tm:tm + pre, :]

    acc = x_ref[...]
    col = 0
    for branch in (att_ref[...], conv_out, y_ref[...]):
        width = branch.shape[-1]
        nb = (_rms(branch) * gb_ref[:, col:col + width]).astype(BF16)
        acc = acc + jnp.dot(nb, wout_ref[col:col + width, :], preferred_element_type=F32)
        col += width
    xo_ref[...] = acc


def _mix_out(att, cb, cc, cx, zstate, y_arr, y_spec, x, gb, wout, cw, cbias, tm, halo, row_stride, tiles_per_seq,
             zstate_spec):
    n = x.shape[0]
    pre = zstate_spec.block_shape[-2]
    rows = lambda w: pl.BlockSpec((tm, w), lambda i: (i, 0))
    in_specs = [rows(W_ATT), rows(W_CONV), rows(W_CONV), rows(W_CONV)]
    args = [att, cb, cc, cx]
    if halo:
        per = tm // SUBLANES
        halo_spec = pl.BlockSpec((SUBLANES, W_CONV), lambda i: (jnp.maximum(i * per - 1, 0), 0))
        in_specs += [halo_spec, halo_spec]
        args += [cc, cx]
    in_specs += [zstate_spec, y_spec, rows(D_MODEL), _full(gb.shape), _full(wout.shape), _full(cw.shape),
                 _full(cbias.shape)]
    args += [zstate, y_arr, x, gb, wout, cw, cbias]
    n_tiles = n // tm
    return pl.pallas_call(
        functools.partial(_mix_out_kernel, tm=tm, halo=halo, row_stride=row_stride, tiles_per_seq=tiles_per_seq),
        grid=(n_tiles,),
        in_specs=in_specs,
        out_specs=[rows(D_MODEL), pl.BlockSpec((pre, W_CONV), lambda i: (i, 0))],
        out_shape=[jax.ShapeDtypeStruct((n, D_MODEL), F32), jax.ShapeDtypeStruct((n_tiles * pre, W_CONV), F32)],
        scratch_shapes=[pltpu.VMEM((tm + pre, W_CONV), F32)],
        compiler_params=_cparams(("parallel",)),
    )(*args)


def _ffn_kernel(*refs, moe, final):
    refs = list(refs)
    x_ref, g_ref = refs[:2]
    pos = 2
    if moe:
        wr_ref, br_ref = refs[pos:pos + 2]
        pos += 2
    wg_ref, wu_ref, wd_ref = refs[pos:pos + 3]
    pos += 3
    if final:
        gf_ref = refs[pos]
        pos += 1
    o_ref, h_sc, acc_sc = refs[pos:pos + 3]
    pos += 3
    if moe:
        comb_sc = refs[pos]
    e = pl.program_id(1)
    c = pl.program_id(2)
    last = (e == pl.num_programs(1) - 1) & (c == pl.num_programs(2) - 1)

    @pl.when((e == 0) & (c == 0))
    def _():
        h = _rms(x_ref[...]) * g_ref[...]
        h_sc[...] = h.astype(BF16)
        acc_sc[...] = jnp.zeros_like(acc_sc)
        if moe:
            logits = _dot_split(h, wr_ref[...]) + br_ref[...]
            lane = lax.broadcasted_iota(jnp.int32, logits.shape, 1)
            logits = jnp.where(lane < N_EXPERTS, logits, -jnp.inf)
            keep = _top_k_mask(logits, 0, N_EXPERTS, EXPERT_TOP_K) & (lane < N_EXPERTS)
            top = jnp.max(logits, axis=1, keepdims=True)
            w = jnp.where(keep, jnp.exp(logits - top), 0.0)
            comb_sc[...] = w / jnp.sum(w, axis=1, keepdims=True)

    hb = h_sc[...]
    a = jnp.dot(hb, wg_ref[...], preferred_element_type=F32)
    b = jnp.dot(hb, wu_ref[...], preferred_element_type=F32)
    t = (a * jax.nn.sigmoid(a) * b).astype(BF16)
    d = jnp.dot(t, wd_ref[...], preferred_element_type=F32)
    if moe:
        d = d * _lane_column(comb_sc[...], e)
    acc_sc[...] += d

    @pl.when(last)
    def _():
        out = x_ref[...] + acc_sc[...]
        if final:
            out = _rms(out) * gf_ref[...]
        o_ref[...] = out


def _ffn(x, g, wg, wu, wd, tm, f_chunk, router=None, g_final=None):
    n = x.shape[0]
    n_exp, _, f_dim = wg.shape
    moe = router is not None
    final = g_final is not None
    rows = pl.BlockSpec((tm, D_MODEL), lambda i, e, c: (i, 0))
    const = lambda shape: pl.BlockSpec(shape, lambda i, e, c: (0,) * len(shape))
    in_specs = [rows, const((1, D_MODEL))]
    args = [x, g]
    if moe:
        in_specs += [const(router[0].shape), const(router[1].shape)]
        args += list(router)
    in_specs += [pl.BlockSpec((None, D_MODEL, f_chunk), lambda i, e, c: (e, 0, c)),
                 pl.BlockSpec((None, D_MODEL, f_chunk), lambda i, e, c: (e, 0, c)),
                 pl.BlockSpec((None, f_chunk, D_MODEL), lambda i, e, c: (e, c, 0))]
    args += [wg, wu, wd]
    if final:
        in_specs.append(const((1, D_MODEL)))
        args.append(g_final)
    scratch = [pltpu.VMEM((tm, D_MODEL), BF16), pltpu.VMEM((tm, D_MODEL), F32)]
    if moe:
        scratch.append(pltpu.VMEM((tm, LANES), F32))
    return pl.pallas_call(
        functools.partial(_ffn_kernel, moe=moe, final=final),
        grid=(n // tm, n_exp, f_dim // f_chunk),
        in_specs=in_specs,
        out_specs=rows,
        out_shape=jax.ShapeDtypeStruct((n, D_MODEL), F32),
        scratch_shapes=scratch,
        compiler_params=_cparams(("parallel", "arbitrary", "arbitrary")),
    )(*args)


MOE_TILE = 1024
MOE_CHUNKS = (256, 320, 384)


def _router_kernel(x_ref, g_ref, wr_ref, br_ref, h_ref, comb_ref, cnt_ref):
    h = _rms(x_ref[...]) * g_ref[...]
    h_ref[...] = h.astype(BF16)
    logits = _dot_split(h, wr_ref[...]) + br_ref[...]
    lane = lax.broadcasted_iota(jnp.int32, logits.shape, 1)
    logits = jnp.where(lane < N_EXPERTS, logits, -jnp.inf)
    keep = _top_k_mask(logits, 0, N_EXPERTS, EXPERT_TOP_K) & (lane < N_EXPERTS)
    w = jnp.where(keep, jnp.exp(logits - jnp.max(logits, axis=1, keepdims=True)), 0.0)
    comb = w / jnp.sum(w, axis=1, keepdims=True)
    comb_ref[...] = comb
    count = jnp.sum(jnp.where(comb > 0.0, 1.0, 0.0), axis=0, keepdims=True)
    cnt_ref[...] = jnp.broadcast_to(count, cnt_ref.shape)


def _router(x, g, wr, br, tm):
    n = x.shape[0]
    n_tiles = n // tm
    h, comb, cnt = pl.pallas_call(
        _router_kernel,
        grid=(n_tiles,),
        in_specs=[pl.BlockSpec((tm, D_MODEL), lambda i: (i, 0)), _full((1, D_MODEL)), _full(wr.shape), _full(br.shape)],
        out_specs=[pl.BlockSpec((tm, D_MODEL), lambda i: (i, 0)), pl.BlockSpec((tm, LANES), lambda i: (i, 0)),
                   pl.BlockSpec((SUBLANES, LANES), lambda i: (i, 0))],
        out_shape=[jax.ShapeDtypeStruct((n, D_MODEL), BF16), jax.ShapeDtypeStruct((n, LANES), F32),
                   jax.ShapeDtypeStruct((n_tiles * SUBLANES, LANES), F32)],
        compiler_params=_cparams(("parallel",)),
    )(x, g, wr, br)
    counts = cnt.reshape(n_tiles, SUBLANES, LANES)[:, 0, :N_EXPERTS].astype(jnp.int32).reshape(-1)
    return h, comb, counts


def _moe_kernel(cnt_ref, x_ref, h_ref, comb_ref, wg_ref, wu_ref, wd_ref, gf_ref, o_ref,
                tri_sc, rank_sc, maskt_sc, rankt_sc, *, tm, chunks, final):
    i = pl.program_id(0)
    e = pl.program_id(1)
    sub = 256

    @pl.when((i == 0) & (e == 0))
    def _():
        for rc in range(tm // sub):
            r = lax.broadcasted_iota(jnp.int32, (sub, tm), 0) + rc * sub
            c = lax.broadcasted_iota(jnp.int32, (sub, tm), 1)
            tri_sc[rc * sub:(rc + 1) * sub, :] = jnp.where(c < r, 1.0, 0.0).astype(BF16)

    @pl.when(e == 0)
    def _():
        routed = jnp.where(comb_ref[...] > 0.0, 1.0, 0.0).astype(BF16)
        rank_sc[...] = jnp.dot(tri_sc[...], routed, preferred_element_type=F32)
        er = lax.broadcasted_iota(jnp.int32, (LANES, LANES), 0)
        ec = lax.broadcasted_iota(jnp.int32, (LANES, LANES), 1)
        eye = jnp.where(er == ec, 1.0, 0.0).astype(BF16)
        routed_t = lax.dot_general(eye, routed, NT_DIMS, preferred_element_type=F32)
        rank_t = lax.dot_general(routed_t.astype(BF16), tri_sc[...], NT_DIMS, preferred_element_type=F32)
        maskt_sc[...] = routed_t[0:SUBLANES]
        rankt_sc[...] = rank_t[0:SUBLANES]
        o_ref[...] = x_ref[...]

    count = cnt_ref[i * N_EXPERTS + e]
    gate_col = _lane_column(comb_ref[...], e)
    rank_col = _lane_column(rank_sc[...], e)
    routed_row = maskt_sc[pl.ds(e, 1), :]
    rank_row = rankt_sc[pl.ds(e, 1), :]
    def run_chunk(size, base):
        size_pad = -(-size // LANES) * LANES
        slot = lax.broadcasted_iota(jnp.int32, (size, tm), 0).astype(F32) + float(base)
        pick = jnp.where((rank_row == slot) & (routed_row > 0.5), 1.0, 0.0).astype(BF16)
        xg = jnp.dot(pick, h_ref[...], preferred_element_type=F32).astype(BF16)
        a = jnp.dot(xg, wg_ref[...], preferred_element_type=F32)
        b = jnp.dot(xg, wu_ref[...], preferred_element_type=F32)
        t = (a * jax.nn.sigmoid(a) * b).astype(BF16)
        y = jnp.dot(t, wd_ref[...], preferred_element_type=F32).astype(BF16)
        if size_pad > size:
            y = jnp.concatenate([y, jnp.zeros((size_pad - size, D_MODEL), BF16)], axis=0)
        slot_l = lax.broadcasted_iota(jnp.int32, (tm, size_pad), 1).astype(F32) + float(base)
        place = jnp.where((rank_col == slot_l) & (gate_col > 0.0), 1.0, 0.0).astype(BF16)
        o_ref[...] += gate_col * jnp.dot(place, y, preferred_element_type=F32)

    largest = chunks[-1]
    for idx, size in enumerate(chunks):
        fits_smaller = chunks[idx - 1] if idx > 0 else 0
        cond = count > fits_smaller
        if size != largest:
            cond = cond & (count <= size)
        pl.when(cond)(functools.partial(run_chunk, size, 0))
    for ch in range(1, -(-tm // largest)):
        pl.when(count > ch * largest)(functools.partial(run_chunk, largest, ch * largest))

    if final:
        @pl.when(e == pl.num_programs(1) - 1)
        def _():
            o_ref[...] = _rms(o_ref[...]) * gf_ref[...]


def _moe_sparse(x, g, wg, wu, wd, router, g_final=None):
    n = x.shape[0]
    tm, chunks = MOE_TILE, tuple(MOE_CHUNKS)
    assert n % tm == 0 and tm % 256 == 0 and list(chunks) == sorted(set(chunks))
    assert all(c % (2 * SUBLANES) == 0 for c in chunks)
    n_exp, _, f_dim = wg.shape
    assert n_exp <= SUBLANES
    h, comb, counts = _router(x, g, router[0], router[1], tm)
    final = g_final is not None
    gf = g_final if final else jnp.ones((1, D_MODEL), F32)
    rows = lambda w: pl.BlockSpec((tm, w), lambda i, e, cnt: (i, 0))
    grid_spec = pltpu.PrefetchScalarGridSpec(
        num_scalar_prefetch=1,
        grid=(n // tm, n_exp),
        in_specs=[rows(D_MODEL), rows(D_MODEL), rows(LANES),
                  pl.BlockSpec((None, D_MODEL, f_dim), lambda i, e, cnt: (e, 0, 0)),
                  pl.BlockSpec((None, D_MODEL, f_dim), lambda i, e, cnt: (e, 0, 0)),
                  pl.BlockSpec((None, f_dim, D_MODEL), lambda i, e, cnt: (e, 0, 0)),
                  pl.BlockSpec((1, D_MODEL), lambda i, e, cnt: (0, 0))],
        out_specs=rows(D_MODEL),
        scratch_shapes=[pltpu.VMEM((tm, tm), BF16),
                        pltpu.VMEM((tm, LANES), F32),
                        pltpu.VMEM((SUBLANES, tm), F32),
                        pltpu.VMEM((SUBLANES, tm), F32)],
    )
    return pl.pallas_call(
        functools.partial(_moe_kernel, tm=tm, chunks=chunks, final=final),
        grid_spec=grid_spec,
        out_shape=jax.ShapeDtypeStruct((n, D_MODEL), F32),
        compiler_params=_cparams(("arbitrary", "arbitrary")),
    )(counts, x, h, comb, wg, wu, wd, gf)


def _alibi_slopes_np():
    return (2.0 ** (-8.0 * np.arange(1, N_HEADS + 1) / N_HEADS)).astype(np.float32)


def _ssm_params(a_re, a_im, log_dt, b_re, b_im, c_re, c_im):
    ar, ai = a_re.astype(F32), a_im.astype(F32)
    dt = jnp.exp(log_dt.astype(F32))[:, None]
    mag = jnp.exp(dt * ar)
    abr, abi = mag * jnp.cos(dt * ai), mag * jnp.sin(dt * ai)
    den = ar * ar + ai * ai
    zr = ((abr - 1.0) * ar + abi * ai) / den
    zi = (abi * ar - (abr - 1.0) * ai) / den
    br, bim = b_re.astype(F32), b_im.astype(F32)
    bbr = zr[..., None] * br - zi[..., None] * bim
    bbi = zr[..., None] * bim + zi[..., None] * br
    eye = jnp.eye(N_GROUPS, dtype=F32)
    to_in = lambda m: jnp.einsum('gnc,gh->gchn', m, eye).reshape(W_SSM, N_STATE)
    to_out = lambda m: jnp.einsum('gcn,gh->gnhc', m, eye).reshape(N_STATE, W_SSM)
    bbd = jnp.concatenate([to_in(bbr), to_in(bbi)], axis=1).astype(BF16)
    cbd = jnp.concatenate([to_out(c_re.astype(F32)), -to_out(c_im.astype(F32))], axis=0).astype(BF16)
    a = jnp.stack([abr.reshape(N_STATE), abi.reshape(N_STATE)])
    return a, bbd, cbd


def _trunk(x, w, depth, bsz, seq, time_major, attend, conv0, h0r, h0i):
    n = bsz * seq
    ks, vs, zs, hrs, his = [], [], [], [], []
    if time_major:
        tm, tiles_per_seq, row_stride, halo = n, 1, bsz, False
        pre = (CONV_WIDTH - 1) * bsz
        u_shape = (n, W_SSM)
        u_spec = pl.BlockSpec((tm, W_SSM), lambda i: (0, 0))
        kv_shape = (n, W_ATT)
        kv_spec = pl.BlockSpec((tm, W_ATT), lambda i: (0, 0))
        zstate_spec = pl.BlockSpec((pre, W_CONV), lambda i: (0, 0))
        t_chunk = seq
    else:
        tm, row_stride, halo = ROW_TILE, 1, True
        tiles_per_seq = seq // tm
        pre = SUBLANES
        u_shape = (seq, bsz * W_SSM)
        u_spec = pl.BlockSpec((tm, W_SSM), lambda i: (i % tiles_per_seq, i // tiles_per_seq))
        kv_shape = (depth, bsz, W_ATT, seq)
        zstate_spec = pl.BlockSpec((None, pre, W_CONV), lambda i: (i // tiles_per_seq, 0, 0))
        t_chunk = 32
    k, v = (None, None) if time_major else (jnp.zeros(kv_shape, F32), jnp.zeros(kv_shape, F32))
    for l in range(depth):
        if not time_major:
            kv_spec = pl.BlockSpec((None, None, W_ATT, tm),
                                   lambda i, l=l: (l, i // tiles_per_seq, 0, i % tiles_per_seq))
        q, k, v, cb, cc, cx, u = _mix_in(x, w['g_mix'][l], w['w_in'][l], w['w_kvt'][l], tm, u_shape, u_spec,
                                         kv_shape, kv_spec, kv_prev=None if time_major else (k, v))
        att = attend(l, q, k, v)
        y, hr, hi = _ssm(u, h0r[l], h0i[l], *w['ssm'][l], bsz, seq, t_chunk)
        x, zlast = _mix_out(att, cb, cc, cx, conv0[l], y, u_spec, x, w['g_branch'][l], w['w_out'][l],
                            w['conv_w'][l], w['conv_b'][l], tm, halo, row_stride, tiles_per_seq, zstate_spec)
        g_final = w['g_final'] if l == depth - 1 else None
        if l % 2 == 0:
            x = _ffn(x, w['g_ffn'][l], *w['dense'][l // 2], tm, w['dense'][l // 2][0].shape[-1] // 2, g_final=g_final)
        else:
            wg, wu, wd, router = w['moe'][l // 2]
            if x.shape[0] % MOE_TILE == 0:
                x = _moe_sparse(x, w['g_ffn'][l], wg, wu, wd, router, g_final=g_final)
            else:
                x = _ffn(x, w['g_ffn'][l], wg, wu, wd, tm, wg.shape[-1], router=router, g_final=g_final)
        ks.append(k)
        vs.append(v)
        zs.append(zlast)
        hrs.append(hr)
        his.append(hi)
    return x, ks, vs, zs, hrs, his


def kernel(x_prompt, x_sample, cache_k, cache_v, state_conv, state_ssm_re, state_ssm_im, page_table, g_mix_norm, w_in, g_branch, w_out, conv_w, conv_b, ssm_a_re, ssm_a_im, ssm_log_dt, ssm_b_re, ssm_b_im, ssm_c_re, ssm_c_im, ssm_d, ssm_w_glu, g_ffn_norm, w_ffn_gate, w_ffn_up, w_ffn_down, w_router, b_router, w_exp_gate, w_exp_up, w_exp_down, g_final):
    depth = w_in.shape[0]
    bp, sp, _ = x_prompt.shape
    bs, ss, _ = x_sample.shape
    n_pool, page = cache_k.shape[1], cache_k.shape[2]
    slopes_np = _alibi_slopes_np()
    slopes = jnp.asarray(slopes_np)

    row = lambda a: a.reshape(a.shape[0], 1, a.shape[-1])
    w_in_bf16 = w_in.astype(BF16)
    w = {
        'g_mix': row(g_mix_norm), 'w_in': w_in_bf16, 'g_branch': row(g_branch), 'w_out': w_out.astype(BF16),
        'w_kvt': jnp.swapaxes(w_in_bf16[:, :, W_ATT:3 * W_ATT], 1, 2),
        'conv_w': conv_w, 'conv_b': row(conv_b), 'g_ffn': row(g_ffn_norm), 'g_final': g_final.reshape(1, D_MODEL),
        'ssm': [(*_ssm_params(ssm_a_re[l], ssm_a_im[l], ssm_log_dt[l], ssm_b_re[l], ssm_b_im[l], ssm_c_re[l],
                              ssm_c_im[l]), ssm_d[l].reshape(1, W_SSM), ssm_w_glu[l].astype(BF16))
                for l in range(depth)],
        'dense': [(w_ffn_gate[j:j + 1].astype(BF16), w_ffn_up[j:j + 1].astype(BF16), w_ffn_down[j:j + 1].astype(BF16))
                  for j in range(w_ffn_gate.shape[0])],
        'moe': [(w_exp_gate[j].astype(BF16), w_exp_up[j].astype(BF16), w_exp_down[j].astype(BF16),
                 (jnp.pad(w_router[j], ((0, 0), (0, LANES - N_EXPERTS))),
                  jnp.pad(b_router[j].reshape(1, N_EXPERTS), ((0, 0), (0, LANES - N_EXPERTS)))))
                for j in range(w_exp_gate.shape[0])],
    }

    conv0_p = jnp.zeros((depth, bp, SUBLANES, W_CONV), F32)
    h0_p = jnp.zeros((depth, bp, N_STATE), F32)
    attend_p = lambda l, q, kt, vt: _moba_prompt(q, kt, vt, l, slopes, bp, sp)
    yp, kts, vts, zs, hrs, his = _trunk(x_prompt.reshape(bp * sp, D_MODEL), w, depth, bp, sp, False, attend_p,
                                        conv0_p, h0_p, h0_p)
    y_prompt = yp.reshape(bp, sp, D_MODEL)
    from_t = lambda a: jnp.transpose(a[-1].reshape(depth, bp, N_HEADS, HEAD_DIM, sp), (0, 1, 4, 2, 3))
    k_p, v_p = from_t(kts), from_t(vts)
    tiles = sp // ROW_TILE
    conv_p = jnp.stack(zs).reshape(depth, bp, tiles, SUBLANES, W_CONV)[:, :, -1, SUBLANES - (CONV_WIDTH - 1):, :]
    sre_p = jnp.stack(hrs).reshape(depth, bp, N_GROUPS, SSM_STATE)
    sim_p = jnp.stack(his).reshape(depth, bp, N_GROUPS, SSM_STATE)

    to_t = lambda c: jnp.transpose(c, (0, 1, 3, 4, 2)).reshape(depth, n_pool, W_ATT, page)
    cache_kt, cache_vt = to_t(cache_k), to_t(cache_v)
    xs = jnp.swapaxes(x_sample, 0, 1).reshape(ss * bs, D_MODEL)
    conv0_s = jnp.swapaxes(state_conv, 1, 2).reshape(depth, (CONV_WIDTH - 1) * bs, W_CONV)
    h0r_s = state_ssm_re.reshape(depth, bs, N_STATE)
    h0i_s = state_ssm_im.reshape(depth, bs, N_STATE)
    attend_s = lambda l, q, k, v: _moba_sample(q, k, v, cache_kt, cache_vt, l, page_table, slopes_np, bs, ss)
    ys, ks, vs, zs, hrs, his = _trunk(xs, w, depth, bs, ss, True, attend_s, conv0_s, h0r_s, h0i_s)
    from_tb = lambda a, width: jnp.swapaxes(a.reshape(-1, ss, bs, width), 1, 2)
    y_sample = from_tb(ys, D_MODEL)[0]
    k_s = from_tb(jnp.stack(ks), W_ATT).reshape(depth, bs, ss, N_HEADS, HEAD_DIM)
    v_s = from_tb(jnp.stack(vs), W_ATT).reshape(depth, bs, ss, N_HEADS, HEAD_DIM)
    conv_s = jnp.swapaxes(jnp.stack(zs).reshape(depth, CONV_WIDTH - 1, bs, W_CONV), 1, 2)
    sre_s = jnp.stack(hrs).reshape(depth, bs, N_GROUPS, SSM_STATE)
    sim_s = jnp.stack(his).reshape(depth, bs, N_GROUPS, SSM_STATE)
    return (y_prompt, y_sample, k_p, v_p, conv_p, sre_p, sim_p, k_s, v_s, conv_s, sre_s, sim_s)
```

```python
import functools

import numpy as np
import jax
import jax.numpy as jnp
from jax import lax
from jax.experimental import pallas as pl
from jax.experimental.pallas import tpu as pltpu

F32 = jnp.float32
BF16 = jnp.bfloat16
HIGHEST = lax.Precision.HIGHEST

D_MODEL = 1024
N_HEADS = 8
HEAD_DIM = 64
W_ATT = N_HEADS * HEAD_DIM
W_CONV = 256
W_SSM = 256
CONV_WIDTH = 3
SSM_GROUP = 16
N_GROUPS = 16
SSM_STATE = 64
N_STATE = N_GROUPS * SSM_STATE
MOBA_BLOCK = 256
MOBA_TOP_K = 3
ATT_SCALE = HEAD_DIM ** -0.5
N_EXPERTS = 8
EXPERT_TOP_K = 2
RMS_EPS = 1e-6
NEG = -1e30

LANES = 128
SUBLANES = 8
VMEM_LIMIT = 56 * 1024 * 1024
ROW_TILE = 512
NT_DIMS = (((1,), (1,)), ((), ()))


def _cparams(sem):
    return pltpu.CompilerParams(dimension_semantics=sem, vmem_limit_bytes=VMEM_LIMIT)


def _rms(x):
    return x * lax.rsqrt(jnp.mean(x * x, axis=-1, keepdims=True) + RMS_EPS)


def _full(shape):
    n = len(shape)
    return pl.BlockSpec(shape, lambda *_: (0,) * n)


def _split_bf16(x):
    hi = x.astype(BF16)
    return hi, (x - hi.astype(F32)).astype(BF16)


def _dot_split(a, b):
    a_hi, a_lo = _split_bf16(a)
    b_hi, b_lo = _split_bf16(b)
    dot = functools.partial(jnp.dot, preferred_element_type=F32)
    return dot(a_hi, b_hi) + (dot(a_hi, b_lo) + dot(a_lo, b_hi))


def _lane_column(x, idx):
    lane = lax.broadcasted_iota(jnp.int32, x.shape, 1)
    return jnp.sum(jnp.where(lane == idx, x, 0.0), axis=1, keepdims=True)


def _top_k_mask(score, first, n_cand, k):
    lane = lax.broadcasted_iota(jnp.int32, score.shape, 1)
    rank = jnp.zeros(score.shape, jnp.int32)
    for m in range(first, first + n_cand):
        sm = score[:, m:m + 1]
        rank += ((sm > score) | ((sm == score) & (m < lane))).astype(jnp.int32)
    return rank < k


def _mix_in_kernel(x_ref, g_ref, w_ref, wkvt_ref, *rest, kv_transposed):
    q_ref, k_ref, v_ref, cb_ref, cc_ref, cx_ref, u_ref = rest[-7:]
    h = (_rms(x_ref[...]) * g_ref[...]).astype(BF16)
    outs = (q_ref, k_ref, v_ref, cb_ref, cc_ref, cx_ref, u_ref)
    widths = (W_ATT,) * 3 + (W_CONV,) * 3 + (W_SSM,)
    col = 0
    for idx, (ref, width) in enumerate(zip(outs, widths)):
        if kv_transposed and idx in (1, 2):
            ref[...] = lax.dot_general(wkvt_ref[(idx - 1) * W_ATT:idx * W_ATT, :], h, NT_DIMS,
                                       preferred_element_type=F32)
        else:
            ref[...] = jnp.dot(h, w_ref[:, col:col + width], preferred_element_type=F32)
        col += width


def _mix_in(x, g, w_bf16, wkvt_bf16, tm, u_shape, u_spec, kv_shape, kv_spec, kv_prev=None):
    n = x.shape[0]
    rows = lambda w: pl.BlockSpec((tm, w), lambda i: (i, 0))
    conv_widths = (W_CONV, W_CONV, W_CONV)
    in_specs = [rows(D_MODEL), _full((1, D_MODEL)), _full(w_bf16.shape), _full(wkvt_bf16.shape)]
    args = [x, g, w_bf16, wkvt_bf16]
    aliases = {}
    if kv_prev is not None:
        in_specs += [pl.BlockSpec(memory_space=pl.ANY)] * 2
        args += list(kv_prev)
        aliases = {4: 1, 5: 2}
    return pl.pallas_call(
        functools.partial(_mix_in_kernel, kv_transposed=len(kv_shape) == 4),
        grid=(n // tm,),
        in_specs=in_specs,
        out_specs=[rows(W_ATT), kv_spec, kv_spec] + [rows(w) for w in conv_widths] + [u_spec],
        out_shape=[jax.ShapeDtypeStruct((n, W_ATT), F32), jax.ShapeDtypeStruct(kv_shape, F32),
                   jax.ShapeDtypeStruct(kv_shape, F32)]
                  + [jax.ShapeDtypeStruct((n, w), F32) for w in conv_widths] + [jax.ShapeDtypeStruct(u_shape, F32)],
        input_output_aliases=aliases,
        compiler_params=_cparams(("parallel",)),
    )(*args)


AUG_KPOS = 32


def _moba_prompt_consts(seq):
    n_blocks = seq // MOBA_BLOCK
    kc = np.zeros((2, 2 * HEAD_DIM, seq), np.float32)
    vc = np.zeros((2, 2 * HEAD_DIM, seq), np.float32)
    for hl in range(2):
        aug = HEAD_DIM * (1 - hl)
        for n in range(n_blocks):
            kc[hl, aug + n, n * MOBA_BLOCK:(n + 1) * MOBA_BLOCK] = 1.0
        kc[hl, aug + AUG_KPOS, :] = np.arange(seq) % MOBA_BLOCK
        vc[hl, aug, :] = 1.0
    return jnp.asarray(kc), jnp.asarray(vc)


def _moba_prompt_kernel(slopes_ref, q_ref, kt_ref, vt_ref, kc_ref, vc_ref, o_ref, kb_sc, vb_sc, *, n_blocks):
    j = pl.program_id(1)
    blk = MOBA_BLOCK
    pair = 2 * HEAD_DIM
    kt = kt_ref[...]
    vt = vt_ref[...]
    row = lax.broadcasted_iota(jnp.int32, kt.shape, 0)
    for hl in range(2):
        own_rows = (row >= HEAD_DIM * hl) & (row < HEAD_DIM * (hl + 1))
        kb_sc[hl] = jnp.where(own_rows, kt, kc_ref[hl]).astype(BF16)
        vb_sc[hl] = jnp.where(own_rows, vt, vc_ref[hl]).astype(BF16)

    krow = lax.broadcasted_iota(jnp.int32, (pair, LANES), 0)
    klane = lax.broadcasted_iota(jnp.int32, (pair, LANES), 1)
    aug_of_row = jnp.where(krow < HEAD_DIM, HEAD_DIM, 0)
    kmean = jnp.zeros((pair, LANES), F32)
    for n in range(n_blocks):
        mean_n = jnp.mean(kt[:, n * blk:(n + 1) * blk], axis=1, keepdims=True)
        kmean = jnp.where(klane == aug_of_row + n, mean_n, kmean)
    kmean_hi, kmean_lo = _split_bf16(kmean)

    lane = lax.broadcasted_iota(jnp.int32, (blk, pair), 1)
    r_io = lax.broadcasted_iota(jnp.int32, (blk, blk), 0)
    c_io = lax.broadcasted_iota(jnp.int32, (blk, blk), 1)
    causal = c_io <= r_io
    for i in range(n_blocks):
        q_i = q_ref[i * blk:(i + 1) * blk, :]
        past, width = i * blk, (i + 1) * blk
        if i > MOBA_TOP_K:
            q_hi, q_lo = _split_bf16(q_i)
            dot = functools.partial(jnp.dot, preferred_element_type=F32)
            gate = dot(q_hi, kmean_hi) + (dot(q_hi, kmean_lo) + dot(q_lo, kmean_hi))
        outs = []
        for hl in range(2):
            aug = HEAD_DIM * (1 - hl)
            slope = slopes_ref[2 * j + hl]
            qh = jnp.where((lane >= HEAD_DIM * hl) & (lane < HEAD_DIM * (hl + 1)), q_i, 0.0)
            is_past = (lane >= aug) & (lane < aug + i)
            if i > MOBA_TOP_K:
                keep =_top_k_mask(jnp.where(is_past, gate, -jnp.inf), aug, i, MOBA_TOP_K) & is_past
            else:
                keep = is_past
            bias = jnp.where(keep | (lane == aug + i), 0.0, NEG) + slope * (blk * (lane - (aug + i)).astype(F32))
            q_aug = jnp.where((lane >= aug) & (lane <= aug + i), bias, qh * ATT_SCALE)
            q_aug = jnp.where(lane == aug + AUG_KPOS, slope, q_aug).astype(BF16)

            s = jnp.dot(q_aug, kb_sc[hl, :, 0:width], preferred_element_type=F32)
            s_own = jnp.where(causal, s[:, past:width], NEG)
            s = jnp.concatenate([s[:, 0:past], s_own], axis=1) if i > 0 else s_own
            m = jnp.max(s, axis=1, keepdims=True)
            o = lax.dot_general(jnp.exp(s - m).astype(BF16), vb_sc[hl, :, 0:width], NT_DIMS,
                                preferred_element_type=F32)
            outs.append(o * (1.0 / o[:, aug:aug + 1]))
        o_ref[i * blk:(i + 1) * blk, :] = jnp.where(lane < HEAD_DIM, outs[0], outs[1])


def _moba_prompt(q, kt, vt, layer, slopes, bsz, seq):
    assert seq % MOBA_BLOCK == 0
    n_blocks = seq // MOBA_BLOCK
    assert n_blocks <= AUG_KPOS < HEAD_DIM
    pair = 2 * HEAD_DIM
    kc, vc = _moba_prompt_consts(seq)
    kv_spec = pl.BlockSpec((None, None, pair, seq), lambda b, j: (layer, b, j, 0))
    qo_spec = pl.BlockSpec((None, seq, pair), lambda b, j: (b, 0, j))
    const_spec = pl.BlockSpec((2, pair, seq), lambda b, j: (0, 0, 0))
    out = pl.pallas_call(
        functools.partial(_moba_prompt_kernel, n_blocks=n_blocks),
        grid=(bsz, W_ATT // pair),
        in_specs=[pl.BlockSpec(memory_space=pltpu.SMEM), qo_spec, kv_spec, kv_spec, const_spec, const_spec],
        out_specs=qo_spec,
        out_shape=jax.ShapeDtypeStruct((bsz, seq, W_ATT), F32),
        scratch_shapes=[pltpu.VMEM((2, pair, seq), BF16), pltpu.VMEM((2, pair, seq), BF16)],
        compiler_params=_cparams(("parallel", "parallel")),
    )(slopes, q.reshape(bsz, seq, W_ATT), kt, vt, kc, vc)
    return out.reshape(bsz * seq, W_ATT)


PAGES_PER_STEP = 32


def _moba_sample_kernel(pt_ref, slope_ref, tq_ref, q_ref, kn_ref, vn_ref, *rest, n_pages, page, t_new):
    del pt_ref
    k_refs = rest[:PAGES_PER_STEP]
    v_refs = rest[PAGES_PER_STEP:2 * PAGES_PER_STEP]
    o_ref, st_sc, selb_sc, ksum_sc, qbd_sc, qbdf_sc, own_sc, inv_sc, acc_sc = rest[2 * PAGES_PER_STEP:]
    ph = pl.program_id(1)
    pg = pl.program_id(2)
    n_groups = n_pages // PAGES_PER_STEP
    ppb = MOBA_BLOCK // page
    n_blocks = n_pages // ppb
    past = n_pages * page
    ht = N_HEADS * t_new

    @pl.when((ph == 0) & (pg == 0))
    def _():
        qt = jnp.concatenate([q_ref[...]] * N_HEADS, axis=0)
        row = lax.broadcasted_iota(jnp.int32, (ht, W_ATT), 0)
        col = lax.broadcasted_iota(jnp.int32, (ht, W_ATT), 1)
        qbd = jnp.where(col // HEAD_DIM == row // t_new, qt, 0.0) * ATT_SCALE
        qbdf_sc[...] = qbd
        qbd_sc[...] = qbd.astype(BF16)
        ksum_sc[...] = jnp.zeros_like(ksum_sc)

    @pl.when(ph == 0)
    def _():
        lane = lax.broadcasted_iota(jnp.int32, (W_ATT, LANES), 1)
        for bb in range(PAGES_PER_STEP // ppb):
            tot = None
            for pp in range(ppb):
                ii = bb * ppb + pp
                kp = k_refs[ii][...]
                tot = kp if tot is None else tot + kp
                st_sc[pg * PAGES_PER_STEP + ii] = jnp.dot(qbd_sc[...], kp.astype(BF16), preferred_element_type=F32)
            blk_idx = pg * (PAGES_PER_STEP // ppb) + bb
            ksum_sc[...] = jnp.where(lane == blk_idx, jnp.sum(tot, axis=1, keepdims=True), ksum_sc[...])

    @pl.when((ph == 0) & (pg == n_groups - 1))
    def _():
        slope = slope_ref[...]
        tq = tq_ref[...]
        lane = lax.broadcasted_iota(jnp.int32, (ht, LANES), 1)
        lane_f = lane.astype(F32)
        gate = jnp.dot(qbdf_sc[...], ksum_sc[...], precision=HIGHEST, preferred_element_type=F32)
        gate = jnp.where(lane < n_blocks, gate, -jnp.inf)
        keep = _top_k_mask(gate, 0, n_blocks, min(MOBA_TOP_K, n_blocks)) & (lane < n_blocks)
        sel_bias = jnp.where(keep, 0.0, NEG)
        for n in range(n_blocks):
            selb_sc[n] = jnp.broadcast_to(sel_bias[:, n:n + 1], (ht, LANES))

        k_own = jnp.concatenate([kn_ref[...], jnp.zeros((LANES - t_new, W_ATT), F32)], axis=0)
        so = lax.dot_general(qbd_sc[...], k_own.astype(BF16), NT_DIMS, preferred_element_type=F32)
        lo = jnp.where((lane_f <= tq) & (lane < t_new), so - slope * (tq - lane_f), NEG)
        qpos = tq + float(past)

        def logits(n, pp):
            pidx = n * ppb + pp
            dist = qpos - (lane_f + jnp.asarray(pidx * page, F32))
            return pidx, st_sc[pidx] - slope * dist + selb_sc[n]

        def max_body(n, m_vec):
            for pp in range(ppb):
                m_vec = jnp.maximum(m_vec, logits(n, pp)[1])
            return m_vec

        m = jnp.max(lax.fori_loop(0, n_blocks, max_body, lo, unroll=4), axis=1, keepdims=True)
        eo = jnp.exp(lo - m)

        def exp_body(n, l_vec):
            for pp in range(ppb):
                pidx, lg = logits(n, pp)
                e = jnp.exp(lg - m)
                st_sc[pidx] = e
                l_vec = l_vec + e
            return l_vec

        l_vec = lax.fori_loop(0, n_blocks, exp_body, eo, unroll=4)
        inv = 1.0 / jnp.sum(l_vec, axis=1, keepdims=True)
        inv_sc[...] = jnp.broadcast_to(inv, inv_sc.shape)
        own_sc[...] = eo * inv
        acc_sc[...] = jnp.zeros_like(acc_sc)

    @pl.when(ph == 1)
    def _():
        inv = inv_sc[...]
        tot = None
        for ii in range(PAGES_PER_STEP):
            p = (st_sc[pg * PAGES_PER_STEP + ii] * inv).astype(BF16)
            d = lax.dot_general(p, v_refs[ii][...].astype(BF16), NT_DIMS, preferred_element_type=F32)
            tot = d if tot is None else tot + d
        acc_sc[...] += tot

    @pl.when((ph == 1) & (pg == n_groups - 1))
    def _():
        v_own = jnp.concatenate([vn_ref[...], jnp.zeros((LANES - t_new, W_ATT), F32)], axis=0)
        acc = acc_sc[...] + jnp.dot(own_sc[...].astype(BF16), v_own.astype(BF16), preferred_element_type=F32)
        col = lax.broadcasted_iota(jnp.int32, (t_new, W_ATT), 1)
        out = jnp.zeros((t_new, W_ATT), F32)
        for h in range(N_HEADS):
            out += jnp.where(col // HEAD_DIM == h, acc[h * t_new:(h + 1) * t_new, :], 0.0)
        o_ref[...] = out


def _moba_sample(q, k, v, cache_kt, cache_vt, layer, page_table, slopes_np, bsz, t_new):
    page = cache_kt.shape[-1]
    n_pages = page_table.shape[1]
    assert MOBA_BLOCK % page == 0 and (n_pages * page) % MOBA_BLOCK == 0 and n_pages % PAGES_PER_STEP == 0
    assert t_new % SUBLANES == 0 and t_new <= LANES and page == LANES
    ht = N_HEADS * t_new
    n_groups = n_pages // PAGES_PER_STEP
    n_blocks = n_pages * page // MOBA_BLOCK
    assert n_blocks <= LANES
    row_h = np.arange(ht) // t_new
    slope_r = jnp.asarray(np.broadcast_to(slopes_np[row_h][:, None], (ht, LANES)), F32)
    tq_r = jnp.asarray(np.broadcast_to((np.arange(ht) % t_new)[:, None], (ht, LANES)), F32)
    wide = lambda a: a.reshape(t_new, bsz * W_ATT)
    tok_spec = pl.BlockSpec((t_new, W_ATT), lambda b, ph, pg, pt: (0, b))
    const_spec = pl.BlockSpec((ht, LANES), lambda b, ph, pg, pt: (0, 0))

    def k_map(ii):
        def f(b, ph, pg, pt):
            p = jnp.where(ph == 0, pg, n_groups - 1) * PAGES_PER_STEP + ii
            return (layer, pt[b * n_pages + p], 0, 0)
        return f

    def v_map(ii):
        def f(b, ph, pg, pt):
            p = jnp.where(ph == 1, pg, 0) * PAGES_PER_STEP + ii
            return (layer, pt[b * n_pages + p], 0, 0)
        return f

    page_block = (None, None, W_ATT, page)
    grid_spec = pltpu.PrefetchScalarGridSpec(
        num_scalar_prefetch=1,
        grid=(bsz, 2, n_groups),
        in_specs=[const_spec, const_spec, tok_spec, tok_spec, tok_spec]
                 + [pl.BlockSpec(page_block, k_map(ii)) for ii in range(PAGES_PER_STEP)]
                 + [pl.BlockSpec(page_block, v_map(ii)) for ii in range(PAGES_PER_STEP)],
        out_specs=tok_spec,
        scratch_shapes=[pltpu.VMEM((n_pages, ht, page), F32),
                        pltpu.VMEM((n_blocks, ht, LANES), F32),
                        pltpu.VMEM((W_ATT, LANES), F32),
                        pltpu.VMEM((ht, W_ATT), BF16),
                        pltpu.VMEM((ht, W_ATT), F32),
                        pltpu.VMEM((ht, LANES), F32),
                        pltpu.VMEM((ht, LANES), F32),
                        pltpu.VMEM((ht, W_ATT), F32)],
    )
    out = pl.pallas_call(
        functools.partial(_moba_sample_kernel, n_pages=n_pages, page=page, t_new=t_new),
        grid_spec=grid_spec,
        out_shape=jax.ShapeDtypeStruct((t_new, bsz * W_ATT), F32),
        compiler_params=_cparams(("parallel", "arbitrary", "arbitrary")),
    )(page_table.reshape(-1), slope_r, tq_r, wide(q), wide(k), wide(v), *([cache_kt] * PAGES_PER_STEP),
      *([cache_vt] * PAGES_PER_STEP))
    return out.reshape(t_new * bsz, W_ATT)


SCAN_LANES = 256


def _gelu_tanh(x):
    return 0.5 * x * (1.0 + jnp.tanh(np.float32(np.sqrt(2.0 / np.pi)) * (x + 0.044715 * (x * x * x))))


def _ssm_kernel(u_ref, h0r_ref, h0i_ref, a_ref, bbd_ref, cbd_ref, d_ref, wglu_ref,
                y_ref, hr_ref, hi_ref, xs_sc, hr_sc, hi_sc, *wide_sc, bsz, t_chunk):
    step = pl.program_id(0)

    @pl.when(step == 0)
    def _():
        hr_sc[...] = h0r_ref[...]
        hi_sc[...] = h0i_ref[...]

    n_half = W_SSM // LANES
    if wide_sc:
        u_scs, y_scs = wide_sc[:n_half], wide_sc[n_half:]
        for b in range(bsz):
            for hf in range(n_half):
                col = b * W_SSM + hf * LANES
                u_scs[hf][pl.ds(b, t_chunk, stride=bsz), :] = u_ref[:, col:col + LANES]
        u = jnp.concatenate([sc[...] for sc in u_scs], axis=1)
    else:
        u = u_ref[...]
    xs_sc[...] = jnp.dot(u.astype(BF16), bbd_ref[...], preferred_element_type=F32)
    for lc in range(N_STATE // SCAN_LANES):
        re = slice(lc * SCAN_LANES, (lc + 1) * SCAN_LANES)
        im = slice(N_STATE + lc * SCAN_LANES, N_STATE + (lc + 1) * SCAN_LANES)
        ar = jnp.broadcast_to(a_ref[0:1, re], (bsz, SCAN_LANES))
        ai = jnp.broadcast_to(a_ref[1:2, re], (bsz, SCAN_LANES))

        def scan_step(t, carry, re=re, im=im, ar=ar, ai=ai):
            hr, hi = carry
            rows = pl.ds(pl.multiple_of(t * bsz, bsz), bsz)
            nr = ar * hr - ai * hi + xs_sc[rows, re]
            ni = ar * hi + ai * hr + xs_sc[rows, im]
            xs_sc[rows, re] = nr
            xs_sc[rows, im] = ni
            return nr, ni

        hr, hi = lax.fori_loop(0, t_chunk, scan_step, (hr_sc[:, re], hi_sc[:, re]))
        hr_sc[:, re] = hr
        hi_sc[:, re] = hi

    y = jnp.dot(xs_sc[...].astype(BF16), cbd_ref[...], preferred_element_type=F32) + d_ref[...] * u
    y = _gelu_tanh(y)
    y = y * jax.nn.sigmoid(jnp.dot(y.astype(BF16), wglu_ref[...], preferred_element_type=F32))
    if wide_sc:
        for hf in range(n_half):
            y_scs[hf][...] = y[:, hf * LANES:(hf + 1) * LANES]
        for b in range(bsz):
            for hf in range(n_half):
                col = b * W_SSM + hf * LANES
                y_ref[:, col:col + LANES] = y_scs[hf][pl.ds(b, t_chunk, stride=bsz), :]
    else:
        y_ref[...] = y

    @pl.when(step == pl.num_programs(0) - 1)
    def _():
        hr_ref[...] = hr_sc[...]
        hi_ref[...] = hi_sc[...]


def _ssm(u, h0r, h0i, a, bbd, cbd, d, wglu, bsz, seq, t_chunk):
    assert bsz % SUBLANES == 0 and seq % t_chunk == 0 and u.shape[0] in (seq, seq * bsz)
    rows = t_chunk * bsz
    wide = u.shape[0] == seq
    blk = pl.BlockSpec((t_chunk, bsz * W_SSM) if wide else (rows, W_SSM), lambda s: (s, 0))
    st_spec = _full((bsz, N_STATE))
    scratch = [pltpu.VMEM((rows, 2 * N_STATE), F32), pltpu.VMEM((bsz, N_STATE), F32), pltpu.VMEM((bsz, N_STATE), F32)]
    if wide:
        scratch += [pltpu.VMEM((rows, LANES), F32)] * (2 * (W_SSM // LANES))
    return pl.pallas_call(
        functools.partial(_ssm_kernel, bsz=bsz, t_chunk=t_chunk),
        grid=(seq // t_chunk,),
        in_specs=[blk, st_spec, st_spec, _full(a.shape), _full(bbd.shape), _full(cbd.shape), _full(d.shape),
                  _full(wglu.shape)],
        out_specs=[blk, st_spec, st_spec],
        out_shape=[jax.ShapeDtypeStruct(u.shape, F32), jax.ShapeDtypeStruct((bsz, N_STATE), F32),
                   jax.ShapeDtypeStruct((bsz, N_STATE), F32)],
        scratch_shapes=scratch,
        compiler_params=_cparams(("arbitrary",)),
    )(u, h0r, h0i, a, bbd, cbd, d, wglu)


def _mix_out_kernel(*refs, tm, halo, row_stride, tiles_per_seq):
    if halo:
        (att_ref, cb_ref, cc_ref, cx_ref, cch_ref, cxh_ref, zst_ref, y_ref, x_ref, gb_ref, wout_ref, cw_ref,
         cbias_ref, xo_ref, zlast_ref, zs_sc) = refs
    else:
        (att_ref, cb_ref, cc_ref, cx_ref, zst_ref, y_ref, x_ref, gb_ref, wout_ref, cw_ref,
         cbias_ref, xo_ref, zlast_ref, zs_sc) = refs
    pre = zs_sc.shape[0] - tm
    z = cc_ref[...] * cx_ref[...]
    if halo:
        first = pl.program_id(0) % tiles_per_seq == 0
        zs_sc[0:pre, :] = jnp.where(first, zst_ref[...], cch_ref[...] * cxh_ref[...])
    else:
        zs_sc[0:pre, :] = zst_ref[...]
    zs_sc[pre:, :] = z
    yc = cbias_ref[...]
    for tap in range(CONV_WIDTH - 1):
        off = pre - (CONV_WIDTH - 1 - tap) * row_stride
        yc = yc + cw_ref[tap:tap + 1, :] * zs_sc[off:off + tm, :]
    yc = yc + cw_ref[CONV_WIDTH - 1:CONV_WIDTH, :] * z
    conv_out = cb_ref[...] * yc
    zlast_ref[...] = zs_sc[tm:tm + pre, :]

    acc = x_ref[...]
    col = 0
    for branch in (att_ref[...], conv_out, y_ref[...]):
        width = branch.shape[-1]
        nb = (_rms(branch) * gb_ref[:, col:col + width]).astype(BF16)
        acc = acc + jnp.dot(nb, wout_ref[col:col + width, :], preferred_element_type=F32)
        col += width
    xo_ref[...] = acc


def _mix_out(att, cb, cc, cx, zstate, y_arr, y_spec, x, gb, wout, cw, cbias, tm, halo, row_stride, tiles_per_seq,
             zstate_spec):
    n = x.shape[0]
    pre = zstate_spec.block_shape[-2]
    rows = lambda w: pl.BlockSpec((tm, w), lambda i: (i, 0))
    in_specs = [rows(W_ATT), rows(W_CONV), rows(W_CONV), rows(W_CONV)]
    args = [att, cb, cc, cx]
    if halo:
        per = tm // SUBLANES
        halo_spec = pl.BlockSpec((SUBLANES, W_CONV), lambda i: (jnp.maximum(i * per - 1, 0), 0))
        in_specs += [halo_spec, halo_spec]
        args += [cc, cx]
    in_specs += [zstate_spec, y_spec, rows(D_MODEL), _full(gb.shape), _full(wout.shape), _full(cw.shape),
                 _full(cbias.shape)]
    args += [zstate, y_arr, x, gb, wout, cw, cbias]
    n_tiles = n // tm
    return pl.pallas_call(
        functools.partial(_mix_out_kernel, tm=tm, halo=halo, row_stride=row_stride, tiles_per_seq=tiles_per_seq),
        grid=(n_tiles,),
        in_specs=in_specs,
        out_specs=[rows(D_MODEL), pl.BlockSpec((pre, W_CONV), lambda i: (i, 0))],
        out_shape=[jax.ShapeDtypeStruct((n, D_MODEL), F32), jax.ShapeDtypeStruct((n_tiles * pre, W_CONV), F32)],
        scratch_shapes=[pltpu.VMEM((tm + pre, W_CONV), F32)],
        compiler_params=_cparams(("parallel",)),
    )(*args)


def _ffn_kernel(*refs, moe, final):
    refs = list(refs)
    x_ref, g_ref = refs[:2]
    pos = 2
    if moe:
        wr_ref, br_ref = refs[pos:pos + 2]
        pos += 2
    wg_ref, wu_ref, wd_ref = refs[pos:pos + 3]
    pos += 3
    if final:
        gf_ref = refs[pos]
        pos += 1
    o_ref, h_sc, acc_sc = refs[pos:pos + 3]
    pos += 3
    if moe:
        comb_sc = refs[pos]
    e = pl.program_id(1)
    c = pl.program_id(2)
    last = (e == pl.num_programs(1) - 1) & (c == pl.num_programs(2) - 1)

    @pl.when((e == 0) & (c == 0))
    def _():
        h = _rms(x_ref[...]) * g_ref[...]
        h_sc[...] = h.astype(BF16)
        acc_sc[...] = jnp.zeros_like(acc_sc)
        if moe:
            logits = _dot_split(h, wr_ref[...]) + br_ref[...]
            lane = lax.broadcasted_iota(jnp.int32, logits.shape, 1)
            logits = jnp.where(lane < N_EXPERTS, logits, -jnp.inf)
            keep = _top_k_mask(logits, 0, N_EXPERTS, EXPERT_TOP_K) & (lane < N_EXPERTS)
            top = jnp.max(logits, axis=1, keepdims=True)
            w = jnp.where(keep, jnp.exp(logits - top), 0.0)
            comb_sc[...] = w / jnp.sum(w, axis=1, keepdims=True)

    hb = h_sc[...]
    a = jnp.dot(hb, wg_ref[...], preferred_element_type=F32)
    b = jnp.dot(hb, wu_ref[...], preferred_element_type=F32)
    t = (a * jax.nn.sigmoid(a) * b).astype(BF16)
    d = jnp.dot(t, wd_ref[...], preferred_element_type=F32)
    if moe:
        d = d * _lane_column(comb_sc[...], e)
    acc_sc[...] += d

    @pl.when(last)
    def _():
        out = x_ref[...] + acc_sc[...]
        if final:
            out = _rms(out) * gf_ref[...]
        o_ref[...] = out


def _ffn(x, g, wg, wu, wd, tm, f_chunk, router=None, g_final=None):
    n = x.shape[0]
    n_exp, f_dim, _ = wd.shape
    up_block = 0
    if wu is None:
        wu, up_block = wg, f_dim // f_chunk
    moe = router is not None
    final = g_final is not None
    rows = pl.BlockSpec((tm, D_MODEL), lambda i, e, c: (i, 0))
    const = lambda shape: pl.BlockSpec(shape, lambda i, e, c: (0,) * len(shape))
    in_specs = [rows, const((1, D_MODEL))]
    args = [x, g]
    if moe:
        in_specs += [const(router[0].shape), const(router[1].shape)]
        args += list(router)
    w_mode = dict(pipeline_mode=pl.Buffered(1)) if n_exp == 1 and f_chunk == f_dim else {}
    in_specs += [pl.BlockSpec((None, D_MODEL, f_chunk), lambda i, e, c: (e, 0, c), **w_mode),
                 pl.BlockSpec((None, D_MODEL, f_chunk), lambda i, e, c: (e, 0, up_block + c), **w_mode),
                 pl.BlockSpec((None, f_chunk, D_MODEL), lambda i, e, c: (e, c, 0), **w_mode)]
    args += [wg, wu, wd]
    if final:
        in_specs.append(const((1, D_MODEL)))
        args.append(g_final)
    scratch = [pltpu.VMEM((tm, D_MODEL), BF16), pltpu.VMEM((tm, D_MODEL), F32)]
    if moe:
        scratch.append(pltpu.VMEM((tm, LANES), F32))
    return pl.pallas_call(
        functools.partial(_ffn_kernel, moe=moe, final=final),
        grid=(n // tm, n_exp, f_dim // f_chunk),
        in_specs=in_specs,
        out_specs=rows,
        out_shape=jax.ShapeDtypeStruct((n, D_MODEL), F32),
        scratch_shapes=scratch,
        compiler_params=_cparams(("parallel", "arbitrary", "arbitrary")),
    )(*args)


MOE_TILE = 1024
MOE_CHUNKS = (256, 320, 384)


def _router_kernel(x_ref, g_ref, wr_ref, br_ref, h_ref, comb_ref, cnt_ref):
    h = _rms(x_ref[...]) * g_ref[...]
    h_ref[...] = h.astype(BF16)
    logits = _dot_split(h, wr_ref[...]) + br_ref[...]
    lane = lax.broadcasted_iota(jnp.int32, logits.shape, 1)
    logits = jnp.where(lane < N_EXPERTS, logits, -jnp.inf)
    keep = _top_k_mask(logits, 0, N_EXPERTS, EXPERT_TOP_K) & (lane < N_EXPERTS)
    w = jnp.where(keep, jnp.exp(logits - jnp.max(logits, axis=1, keepdims=True)), 0.0)
    comb = w / jnp.sum(w, axis=1, keepdims=True)
    comb_ref[...] = comb
    count = jnp.sum(jnp.where(comb > 0.0, 1.0, 0.0), axis=0, keepdims=True)
    cnt_ref[...] = jnp.broadcast_to(count, cnt_ref.shape)


def _router(x, g, wr, br, tm):
    n = x.shape[0]
    n_tiles = n // tm
    h, comb, cnt = pl.pallas_call(
        _router_kernel,
        grid=(n_tiles,),
        in_specs=[pl.BlockSpec((tm, D_MODEL), lambda i: (i, 0)), _full((1, D_MODEL)), _full(wr.shape), _full(br.shape)],
        out_specs=[pl.BlockSpec((tm, D_MODEL), lambda i: (i, 0)), pl.BlockSpec((tm, LANES), lambda i: (i, 0)),
                   pl.BlockSpec((SUBLANES, LANES), lambda i: (i, 0))],
        out_shape=[jax.ShapeDtypeStruct((n, D_MODEL), BF16), jax.ShapeDtypeStruct((n, LANES), F32),
                   jax.ShapeDtypeStruct((n_tiles * SUBLANES, LANES), F32)],
        compiler_params=_cparams(("parallel",)),
    )(x, g, wr, br)
    counts = cnt.reshape(n_tiles, SUBLANES, LANES)[:, 0, :N_EXPERTS].astype(jnp.int32).reshape(-1)
    return h, comb, counts


def _moe_kernel(cnt_ref, x_ref, h_ref, comb_ref, wgu_ref, wd_ref, gf_ref, o_ref,
                tri_sc, rank_sc, maskt_sc, rankt_sc, *, tm, chunks, final):
    i = pl.program_id(0)
    e = pl.program_id(1)
    sub = 256

    @pl.when((i == 0) & (e == 0))
    def _():
        for rc in range(tm // sub):
            r = lax.broadcasted_iota(jnp.int32, (sub, tm), 0) + rc * sub
            c = lax.broadcasted_iota(jnp.int32, (sub, tm), 1)
            tri_sc[rc * sub:(rc + 1) * sub, :] = jnp.where(c < r, 1.0, 0.0).astype(BF16)

    @pl.when(e == 0)
    def _():
        routed = jnp.where(comb_ref[...] > 0.0, 1.0, 0.0).astype(BF16)
        rank_sc[...] = jnp.dot(tri_sc[...], routed, preferred_element_type=F32)
        er = lax.broadcasted_iota(jnp.int32, (LANES, LANES), 0)
        ec = lax.broadcasted_iota(jnp.int32, (LANES, LANES), 1)
        eye = jnp.where(er == ec, 1.0, 0.0).astype(BF16)
        routed_t = lax.dot_general(eye, routed, NT_DIMS, preferred_element_type=F32)
        rank_t = lax.dot_general(routed_t.astype(BF16), tri_sc[...], NT_DIMS, preferred_element_type=F32)
        maskt_sc[...] = routed_t[0:SUBLANES]
        rankt_sc[...] = rank_t[0:SUBLANES]
        o_ref[...] = x_ref[...]

    count = cnt_ref[i * N_EXPERTS + e]
    gate_col = _lane_column(comb_ref[...], e)
    rank_col = _lane_column(rank_sc[...], e)
    routed_row = maskt_sc[pl.ds(e, 1), :]
    rank_row = rankt_sc[pl.ds(e, 1), :]
    def run_chunk(size, base):
        size_pad = -(-size // LANES) * LANES
        slot = lax.broadcasted_iota(jnp.int32, (size, tm), 0).astype(F32) + float(base)
        pick = jnp.where((rank_row == slot) & (routed_row > 0.5), 1.0, 0.0).astype(BF16)
        xg = jnp.dot(pick, h_ref[...], preferred_element_type=F32).astype(BF16)
        ab = jnp.dot(xg, wgu_ref[...], preferred_element_type=F32)
        f_dim = wd_ref.shape[0]
        a, b = ab[:, 0:f_dim], ab[:, f_dim:2 * f_dim]
        t = (a * jax.nn.sigmoid(a) * b).astype(BF16)
        y = jnp.dot(t, wd_ref[...], preferred_element_type=F32).astype(BF16)
        if size_pad > size:
            y = jnp.concatenate([y, jnp.zeros((size_pad - size, D_MODEL), BF16)], axis=0)
        slot_l = lax.broadcasted_iota(jnp.int32, (tm, size_pad), 1).astype(F32) + float(base)
        place = jnp.where((rank_col == slot_l) & (gate_col > 0.0), 1.0, 0.0).astype(BF16)
        o_ref[...] += gate_col * jnp.dot(place, y, preferred_element_type=F32)

    largest = chunks[-1]
    for idx, size in enumerate(chunks):
        fits_smaller = chunks[idx - 1] if idx > 0 else 0
        cond = count > fits_smaller
        if size != largest:
            cond = cond & (count <= size)
        pl.when(cond)(functools.partial(run_chunk, size, 0))
    for ch in range(1, -(-tm // largest)):
        pl.when(count > ch * largest)(functools.partial(run_chunk, largest, ch * largest))

    if final:
        @pl.when(e == pl.num_programs(1) - 1)
        def _():
            o_ref[...] = _rms(o_ref[...]) * gf_ref[...]


def _moe_sparse(x, g, wgu, wd, router, g_final=None):
    n = x.shape[0]
    tm, chunks = MOE_TILE, tuple(MOE_CHUNKS)
    assert n % tm == 0 and tm % 256 == 0 and list(chunks) == sorted(set(chunks))
    assert all(c % (2 * SUBLANES) == 0 for c in chunks)
    n_exp, f_dim, _ = wd.shape
    assert n_exp <= SUBLANES and wgu.shape[-1] == 2 * f_dim
    h, comb, counts = _router(x, g, router[0], router[1], tm)
    final = g_final is not None
    gf = g_final if final else jnp.ones((1, D_MODEL), F32)
    rows = lambda w: pl.BlockSpec((tm, w), lambda i, e, cnt: (i, 0))
    grid_spec = pltpu.PrefetchScalarGridSpec(
        num_scalar_prefetch=1,
        grid=(n // tm, n_exp),
        in_specs=[rows(D_MODEL), rows(D_MODEL), rows(LANES),
                  pl.BlockSpec((None, D_MODEL, 2 * f_dim), lambda i, e, cnt: (e, 0, 0)),
                  pl.BlockSpec((None, f_dim, D_MODEL), lambda i, e, cnt: (e, 0, 0)),
                  pl.BlockSpec((1, D_MODEL), lambda i, e, cnt: (0, 0))],
        out_specs=rows(D_MODEL),
        scratch_shapes=[pltpu.VMEM((tm, tm), BF16),
                        pltpu.VMEM((tm, LANES), F32),
                        pltpu.VMEM((SUBLANES, tm), F32),
                        pltpu.VMEM((SUBLANES, tm), F32)],
    )
    return pl.pallas_call(
        functools.partial(_moe_kernel, tm=tm, chunks=chunks, final=final),
        grid_spec=grid_spec,
        out_shape=jax.ShapeDtypeStruct((n, D_MODEL), F32),
        compiler_params=_cparams(("arbitrary", "arbitrary")),
    )(counts, x, h, comb, wgu, wd, gf)


def _alibi_slopes_np():
    return (2.0 ** (-8.0 * np.arange(1, N_HEADS + 1) / N_HEADS)).astype(np.float32)


def _ssm_params(a_re, a_im, log_dt, b_re, b_im, c_re, c_im):
    ar, ai = a_re.astype(F32), a_im.astype(F32)
    dt = jnp.exp(log_dt.astype(F32))[:, None]
    mag = jnp.exp(dt * ar)
    abr, abi = mag * jnp.cos(dt * ai), mag * jnp.sin(dt * ai)
    den = ar * ar + ai * ai
    zr = ((abr - 1.0) * ar + abi * ai) / den
    zi = (abi * ar - (abr - 1.0) * ai) / den
    br, bim = b_re.astype(F32), b_im.astype(F32)
    bbr = zr[..., None] * br - zi[..., None] * bim
    bbi = zr[..., None] * bim + zi[..., None] * br
    eye = jnp.eye(N_GROUPS, dtype=F32)
    to_in = lambda m: jnp.einsum('gnc,gh->gchn', m, eye).reshape(W_SSM, N_STATE)
    to_out = lambda m: jnp.einsum('gcn,gh->gnhc', m, eye).reshape(N_STATE, W_SSM)
    bbd = jnp.concatenate([to_in(bbr), to_in(bbi)], axis=1).astype(BF16)
    cbd = jnp.concatenate([to_out(c_re.astype(F32)), -to_out(c_im.astype(F32))], axis=0).astype(BF16)
    a = jnp.stack([abr.reshape(N_STATE), abi.reshape(N_STATE)])
    return a, bbd, cbd


def _trunk(x, w, depth, bsz, seq, time_major, attend, conv0, h0r, h0i):
    n = bsz * seq
    ks, vs, zs, hrs, his = [], [], [], [], []
    if time_major:
        tm, tiles_per_seq, row_stride, halo = n, 1, bsz, False
        pre = (CONV_WIDTH - 1) * bsz
        u_shape = (n, W_SSM)
        u_spec = pl.BlockSpec((tm, W_SSM), lambda i: (0, 0))
        kv_shape = (n, W_ATT)
        kv_spec = pl.BlockSpec((tm, W_ATT), lambda i: (0, 0))
        zstate_spec = pl.BlockSpec((pre, W_CONV), lambda i: (0, 0))
        t_chunk = seq
    else:
        tm, row_stride, halo = ROW_TILE, 1, True
        tiles_per_seq = seq // tm
        pre = SUBLANES
        u_shape = (seq, bsz * W_SSM)
        u_spec = pl.BlockSpec((tm, W_SSM), lambda i: (i % tiles_per_seq, i // tiles_per_seq))
        kv_shape = (depth, bsz, W_ATT, seq)
        zstate_spec = pl.BlockSpec((None, pre, W_CONV), lambda i: (i // tiles_per_seq, 0, 0))
        t_chunk = 32
    k, v = (None, None) if time_major else (jnp.zeros(kv_shape, F32), jnp.zeros(kv_shape, F32))
    for l in range(depth):
        if not time_major:
            kv_spec = pl.BlockSpec((None, None, W_ATT, tm),
                                   lambda i, l=l: (l, i // tiles_per_seq, 0, i % tiles_per_seq))
        q, k, v, cb, cc, cx, u = _mix_in(x, w['g_mix'][l], w['w_in'][l], w['w_kvt'][l], tm, u_shape, u_spec,
                                         kv_shape, kv_spec, kv_prev=None if time_major else (k, v))
        att = attend(l, q, k, v)
        y, hr, hi = _ssm(u, h0r[l], h0i[l], *w['ssm'][l], bsz, seq, t_chunk)
        x, zlast = _mix_out(att, cb, cc, cx, conv0[l], y, u_spec, x, w['g_branch'][l], w['w_out'][l],
                            w['conv_w'][l], w['conv_b'][l], tm, halo, row_stride, tiles_per_seq, zstate_spec)
        g_final = w['g_final'] if l == depth - 1 else None
        if l % 2 == 0:
            x = _ffn(x, w['g_ffn'][l], *w['dense'][l // 2], tm, w['dense'][l // 2][0].shape[-1], g_final=g_final)
        else:
            wgu, wd, router = w['moe'][l // 2]
            if x.shape[0] % MOE_TILE == 0:
                x = _moe_sparse(x, w['g_ffn'][l], wgu, wd, router, g_final=g_final)
            else:
                x = _ffn(x, w['g_ffn'][l], wgu, None, wd, tm, wd.shape[1], router=router, g_final=g_final)
        ks.append(k)
        vs.append(v)
        zs.append(zlast)
        hrs.append(hr)
        his.append(hi)
    return x, ks, vs, zs, hrs, his


def kernel(x_prompt, x_sample, cache_k, cache_v, state_conv, state_ssm_re, state_ssm_im, page_table, g_mix_norm, w_in, g_branch, w_out, conv_w, conv_b, ssm_a_re, ssm_a_im, ssm_log_dt, ssm_b_re, ssm_b_im, ssm_c_re, ssm_c_im, ssm_d, ssm_w_glu, g_ffn_norm, w_ffn_gate, w_ffn_up, w_ffn_down, w_router, b_router, w_exp_gate, w_exp_up, w_exp_down, g_final):
    depth = w_in.shape[0]
    bp, sp, _ = x_prompt.shape
    bs, ss, _ = x_sample.shape
    n_pool, page = cache_k.shape[1], cache_k.shape[2]
    slopes_np = _alibi_slopes_np()
    slopes = jnp.asarray(slopes_np)

    row = lambda a: a.reshape(a.shape[0], 1, a.shape[-1])
    w_in_bf16 = w_in.astype(BF16)
    w = {
        'g_mix': row(g_mix_norm), 'w_in': w_in_bf16, 'g_branch': row(g_branch), 'w_out': w_out.astype(BF16),
        'w_kvt': jnp.swapaxes(w_in_bf16[:, :, W_ATT:3 * W_ATT], 1, 2),
        'conv_w': conv_w, 'conv_b': row(conv_b), 'g_ffn': row(g_ffn_norm), 'g_final': g_final.reshape(1, D_MODEL),
        'ssm': [(*_ssm_params(ssm_a_re[l], ssm_a_im[l], ssm_log_dt[l], ssm_b_re[l], ssm_b_im[l], ssm_c_re[l],
                              ssm_c_im[l]), ssm_d[l].reshape(1, W_SSM), ssm_w_glu[l].astype(BF16))
                for l in range(depth)],
        'dense': [(w_ffn_gate[j:j + 1].astype(BF16), w_ffn_up[j:j + 1].astype(BF16), w_ffn_down[j:j + 1].astype(BF16))
                  for j in range(w_ffn_gate.shape[0])],
        'moe': [(jnp.concatenate([w_exp_gate[j], w_exp_up[j]], axis=-1).astype(BF16), w_exp_down[j].astype(BF16),
                 (jnp.pad(w_router[j], ((0, 0), (0, LANES - N_EXPERTS))),
                  jnp.pad(b_router[j].reshape(1, N_EXPERTS), ((0, 0), (0, LANES - N_EXPERTS)))))
                for j in range(w_exp_gate.shape[0])],
    }

    conv0_p = jnp.zeros((depth, bp, SUBLANES, W_CONV), F32)
    h0_p = jnp.zeros((depth, bp, N_STATE), F32)
    attend_p = lambda l, q, kt, vt: _moba_prompt(q, kt, vt, l, slopes, bp, sp)
    yp, kts, vts, zs, hrs, his = _trunk(x_prompt.reshape(bp * sp, D_MODEL), w, depth, bp, sp, False, attend_p,
                                        conv0_p, h0_p, h0_p)
    y_prompt = yp.reshape(bp, sp, D_MODEL)
    from_t = lambda a: jnp.transpose(a[-1].reshape(depth, bp, N_HEADS, HEAD_DIM, sp), (0, 1, 4, 2, 3))
    k_p, v_p = from_t(kts), from_t(vts)
    tiles = sp // ROW_TILE
    conv_p = jnp.stack(zs).reshape(depth, bp, tiles, SUBLANES, W_CONV)[:, :, -1, SUBLANES - (CONV_WIDTH - 1):, :]
    sre_p = jnp.stack(hrs).reshape(depth, bp, N_GROUPS, SSM_STATE)
    sim_p = jnp.stack(his).reshape(depth, bp, N_GROUPS, SSM_STATE)

    to_t = lambda c: jnp.transpose(c, (0, 1, 3, 4, 2)).reshape(depth, n_pool, W_ATT, page)
    cache_kt, cache_vt = to_t(cache_k), to_t(cache_v)
    xs = jnp.swapaxes(x_sample, 0, 1).reshape(ss * bs, D_MODEL)
    conv0_s = jnp.swapaxes(state_conv, 1, 2).reshape(depth, (CONV_WIDTH - 1) * bs, W_CONV)
    h0r_s = state_ssm_re.reshape(depth, bs, N_STATE)
    h0i_s = state_ssm_im.reshape(depth, bs, N_STATE)
    attend_s = lambda l, q, k, v: _moba_sample(q, k, v, cache_kt, cache_vt, l, page_table, slopes_np, bs, ss)
    ys, ks, vs, zs, hrs, his = _trunk(xs, w, depth, bs, ss, True, attend_s, conv0_s, h0r_s, h0i_s)
    from_tb = lambda a, width: jnp.swapaxes(a.reshape(-1, ss, bs, width), 1, 2)
    y_sample = from_tb(ys, D_MODEL)[0]
    k_s = from_tb(jnp.stack(ks), W_ATT).reshape(depth, bs, ss, N_HEADS, HEAD_DIM)
    v_s = from_tb(jnp.stack(vs), W_ATT).reshape(depth, bs, ss, N_HEADS, HEAD_DIM)
    conv_s = jnp.swapaxes(jnp.stack(zs).reshape(depth, CONV_WIDTH - 1, bs, W_CONV), 1, 2)
    sre_s = jnp.stack(hrs).reshape(depth, bs, N_GROUPS, SSM_STATE)
    sim_s = jnp.stack(his).reshape(depth, bs, N_GROUPS, SSM_STATE)
    return (y_prompt, y_sample, k_p, v_p, conv_p, sre_p, sim_p, k_s, v_s, conv_s, sre_s, sim_s)
```

```python
import functools

import numpy as np
import jax
import jax.numpy as jnp
from jax import lax
from jax.experimental import pallas as pl
from jax.experimental.pallas import tpu as pltpu

F32 = jnp.float32
BF16 = jnp.bfloat16
HIGHEST = lax.Precision.HIGHEST

D_MODEL = 1024
N_HEADS = 8
HEAD_DIM = 64
W_ATT = N_HEADS * HEAD_DIM
W_CONV = 256
W_SSM = 256
CONV_WIDTH = 3
SSM_GROUP = 16
N_GROUPS = 16
SSM_STATE = 64
N_STATE = N_GROUPS * SSM_STATE
MOBA_BLOCK = 256
MOBA_TOP_K = 3
ATT_SCALE = HEAD_DIM ** -0.5
N_EXPERTS = 8
EXPERT_TOP_K = 2
RMS_EPS = 1e-6
NEG = -1e30

LANES = 128
SUBLANES = 8
VMEM_LIMIT = 56 * 1024 * 1024
ROW_TILE = 512
NT_DIMS = (((1,), (1,)), ((), ()))


def _cparams(sem):
    return pltpu.CompilerParams(dimension_semantics=sem, vmem_limit_bytes=VMEM_LIMIT)


def _rms(x):
    return x * lax.rsqrt(jnp.mean(x * x, axis=-1, keepdims=True) + RMS_EPS)


def _full(shape):
    n = len(shape)
    return pl.BlockSpec(shape, lambda *_: (0,) * n)


def _split_bf16(x):
    hi = x.astype(BF16)
    return hi, (x - hi.astype(F32)).astype(BF16)


def _dot_split(a, b):
    a_hi, a_lo = _split_bf16(a)
    b_hi, b_lo = _split_bf16(b)
    dot = functools.partial(jnp.dot, preferred_element_type=F32)
    return dot(a_hi, b_hi) + (dot(a_hi, b_lo) + dot(a_lo, b_hi))


def _lane_column(x, idx):
    lane = lax.broadcasted_iota(jnp.int32, x.shape, 1)
    return jnp.sum(jnp.where(lane == idx, x, 0.0), axis=1, keepdims=True)


def _top_k_mask(score, first, n_cand, k):
    lane = lax.broadcasted_iota(jnp.int32, score.shape, 1)
    rank = jnp.zeros(score.shape, jnp.int32)
    for m in range(first, first + n_cand):
        sm = score[:, m:m + 1]
        rank += ((sm > score) | ((sm == score) & (m < lane))).astype(jnp.int32)
    return rank < k


def _mix_in_kernel(x_ref, g_ref, w_ref, wkvt_ref, *rest, kv_transposed):
    q_ref, k_ref, v_ref, cb_ref, cc_ref, cx_ref, u_ref = rest[-7:]
    h = (_rms(x_ref[...]) * g_ref[...]).astype(BF16)
    outs = (q_ref, k_ref, v_ref, cb_ref, cc_ref, cx_ref, u_ref)
    widths = (W_ATT,) * 3 + (W_CONV,) * 3 + (W_SSM,)
    col = 0
    for idx, (ref, width) in enumerate(zip(outs, widths)):
        if kv_transposed and idx in (1, 2):
            ref[...] = lax.dot_general(wkvt_ref[(idx - 1) * W_ATT:idx * W_ATT, :], h, NT_DIMS,
                                       preferred_element_type=F32)
        else:
            ref[...] = jnp.dot(h, w_ref[:, col:col + width], preferred_element_type=F32)
        col += width


def _mix_in(x, g, w_bf16, wkvt_bf16, tm, u_shape, u_spec, kv_shape, kv_spec, kv_prev=None):
    n = x.shape[0]
    rows = lambda w: pl.BlockSpec((tm, w), lambda i: (i, 0))
    conv_widths = (W_CONV, W_CONV, W_CONV)
    in_specs = [rows(D_MODEL), _full((1, D_MODEL)), _full(w_bf16.shape), _full(wkvt_bf16.shape)]
    args = [x, g, w_bf16, wkvt_bf16]
    aliases = {}
    if kv_prev is not None:
        in_specs += [pl.BlockSpec(memory_space=pl.ANY)] * 2
        args += list(kv_prev)
        aliases = {4: 1, 5: 2}
    return pl.pallas_call(
        functools.partial(_mix_in_kernel, kv_transposed=len(kv_shape) == 4),
        grid=(n // tm,),
        in_specs=in_specs,
        out_specs=[rows(W_ATT), kv_spec, kv_spec] + [rows(w) for w in conv_widths] + [u_spec],
        out_shape=[jax.ShapeDtypeStruct((n, W_ATT), F32), jax.ShapeDtypeStruct(kv_shape, F32),
                   jax.ShapeDtypeStruct(kv_shape, F32)]
                  + [jax.ShapeDtypeStruct((n, w), F32) for w in conv_widths] + [jax.ShapeDtypeStruct(u_shape, F32)],
        input_output_aliases=aliases,
        compiler_params=_cparams(("parallel",)),
    )(*args)


AUG_KPOS = 32


def _moba_prompt_consts(seq):
    n_blocks = seq // MOBA_BLOCK
    kc = np.zeros((2, 2 * HEAD_DIM, seq), np.float32)
    vc = np.zeros((2, 2 * HEAD_DIM, seq), np.float32)
    for hl in range(2):
        aug = HEAD_DIM * (1 - hl)
        for n in range(n_blocks):
            kc[hl, aug + n, n * MOBA_BLOCK:(n + 1) * MOBA_BLOCK] = 1.0
        kc[hl, aug + AUG_KPOS, :] = np.arange(seq) % MOBA_BLOCK
        vc[hl, aug, :] = 1.0
    return jnp.asarray(kc), jnp.asarray(vc)


def _moba_prompt_kernel(slopes_ref, q_ref, kt_ref, vt_ref, kc_ref, vc_ref, o_ref, kb_sc, vb_sc, *, n_blocks):
    j = pl.program_id(1)
    blk = MOBA_BLOCK
    pair = 2 * HEAD_DIM
    kt = kt_ref[...]
    vt = vt_ref[...]
    row = lax.broadcasted_iota(jnp.int32, kt.shape, 0)
    for hl in range(2):
        own_rows = (row >= HEAD_DIM * hl) & (row < HEAD_DIM * (hl + 1))
        kb_sc[hl] = jnp.where(own_rows, kt, kc_ref[hl]).astype(BF16)
        vb_sc[hl] = jnp.where(own_rows, vt, vc_ref[hl]).astype(BF16)

    krow = lax.broadcasted_iota(jnp.int32, (pair, LANES), 0)
    klane = lax.broadcasted_iota(jnp.int32, (pair, LANES), 1)
    aug_of_row = jnp.where(krow < HEAD_DIM, HEAD_DIM, 0)
    kmean = jnp.zeros((pair, LANES), F32)
    for n in range(n_blocks):
        mean_n = jnp.mean(kt[:, n * blk:(n + 1) * blk], axis=1, keepdims=True)
        kmean = jnp.where(klane == aug_of_row + n, mean_n, kmean)
    kmean_hi, kmean_lo = _split_bf16(kmean)

    lane = lax.broadcasted_iota(jnp.int32, (blk, pair), 1)
    r_io = lax.broadcasted_iota(jnp.int32, (blk, blk), 0)
    c_io = lax.broadcasted_iota(jnp.int32, (blk, blk), 1)
    causal = c_io <= r_io
    for i in range(n_blocks):
        q_i = q_ref[i * blk:(i + 1) * blk, :]
        past, width = i * blk, (i + 1) * blk
        if i > MOBA_TOP_K:
            q_hi, q_lo = _split_bf16(q_i)
            dot = functools.partial(jnp.dot, preferred_element_type=F32)
            gate = dot(q_hi, kmean_hi) + (dot(q_hi, kmean_lo) + dot(q_lo, kmean_hi))
        outs = []
        for hl in range(2):
            aug = HEAD_DIM * (1 - hl)
            slope = slopes_ref[2 * j + hl]
            qh = jnp.where((lane >= HEAD_DIM * hl) & (lane < HEAD_DIM * (hl + 1)), q_i, 0.0)
            is_past = (lane >= aug) & (lane < aug + i)
            if i > MOBA_TOP_K:
                keep =_top_k_mask(jnp.where(is_past, gate, -jnp.inf), aug, i, MOBA_TOP_K) & is_past
            else:
                keep = is_past
            bias = jnp.where(keep | (lane == aug + i), 0.0, NEG) + slope * (blk * (lane - (aug + i)).astype(F32))
            q_aug = jnp.where((lane >= aug) & (lane <= aug + i), bias, qh * ATT_SCALE)
            q_aug = jnp.where(lane == aug + AUG_KPOS, slope, q_aug).astype(BF16)

            s = jnp.dot(q_aug, kb_sc[hl, :, 0:width], preferred_element_type=F32)
            s_own = jnp.where(causal, s[:, past:width], NEG)
            s = jnp.concatenate([s[:, 0:past], s_own], axis=1) if i > 0 else s_own
            m = jnp.max(s, axis=1, keepdims=True)
            o = lax.dot_general(jnp.exp(s - m).astype(BF16), vb_sc[hl, :, 0:width], NT_DIMS,
                                preferred_element_type=F32)
            outs.append(o * (1.0 / o[:, aug:aug + 1]))
        o_ref[i * blk:(i + 1) * blk, :] = jnp.where(lane < HEAD_DIM, outs[0], outs[1])


def _moba_prompt(q, kt, vt, layer, slopes, bsz, seq):
    assert seq % MOBA_BLOCK == 0
    n_blocks = seq // MOBA_BLOCK
    assert n_blocks <= AUG_KPOS < HEAD_DIM
    pair = 2 * HEAD_DIM
    kc, vc = _moba_prompt_consts(seq)
    kv_spec = pl.BlockSpec((None, None, pair, seq), lambda b, j: (layer, b, j, 0))
    qo_spec = pl.BlockSpec((None, seq, pair), lambda b, j: (b, 0, j))
    const_spec = pl.BlockSpec((2, pair, seq), lambda b, j: (0, 0, 0))
    out = pl.pallas_call(
        functools.partial(_moba_prompt_kernel, n_blocks=n_blocks),
        grid=(bsz, W_ATT // pair),
        in_specs=[pl.BlockSpec(memory_space=pltpu.SMEM), qo_spec, kv_spec, kv_spec, const_spec, const_spec],
        out_specs=qo_spec,
        out_shape=jax.ShapeDtypeStruct((bsz, seq, W_ATT), F32),
        scratch_shapes=[pltpu.VMEM((2, pair, seq), BF16), pltpu.VMEM((2, pair, seq), BF16)],
        compiler_params=_cparams(("parallel", "parallel")),
    )(slopes, q.reshape(bsz, seq, W_ATT), kt, vt, kc, vc)
    return out.reshape(bsz * seq, W_ATT)


PAGES_PER_STEP = 32


def _moba_sample_kernel(pt_ref, slope_ref, tq_ref, q_ref, kn_ref, vn_ref, *rest, n_pages, page, t_new):
    del pt_ref
    k_refs = rest[:PAGES_PER_STEP]
    v_refs = rest[PAGES_PER_STEP:2 * PAGES_PER_STEP]
    o_ref, st_sc, selb_sc, ksum_sc, qbd_sc, qbdf_sc, own_sc, inv_sc, acc_sc = rest[2 * PAGES_PER_STEP:]
    ph = pl.program_id(1)
    pg = pl.program_id(2)
    n_groups = n_pages // PAGES_PER_STEP
    ppb = MOBA_BLOCK // page
    n_blocks = n_pages // ppb
    past = n_pages * page
    ht = N_HEADS * t_new

    @pl.when((ph == 0) & (pg == 0))
    def _():
        qt = jnp.concatenate([q_ref[...]] * N_HEADS, axis=0)
        row = lax.broadcasted_iota(jnp.int32, (ht, W_ATT), 0)
        col = lax.broadcasted_iota(jnp.int32, (ht, W_ATT), 1)
        qbd = jnp.where(col // HEAD_DIM == row // t_new, qt, 0.0) * ATT_SCALE
        qbdf_sc[...] = qbd
        qbd_sc[...] = qbd.astype(BF16)
        ksum_sc[...] = jnp.zeros_like(ksum_sc)

    @pl.when(ph == 0)
    def _():
        lane = lax.broadcasted_iota(jnp.int32, (W_ATT, LANES), 1)
        for bb in range(PAGES_PER_STEP // ppb):
            tot = None
            for pp in range(ppb):
                ii = bb * ppb + pp
                kp = k_refs[ii][...]
                tot = kp if tot is None else tot + kp
                st_sc[pg * PAGES_PER_STEP + ii] = jnp.dot(qbd_sc[...], kp.astype(BF16), preferred_element_type=F32)
            blk_idx = pg * (PAGES_PER_STEP // ppb) + bb
            ksum_sc[...] = jnp.where(lane == blk_idx, jnp.sum(tot, axis=1, keepdims=True), ksum_sc[...])

    @pl.when((ph == 0) & (pg == n_groups - 1))
    def _():
        slope = slope_ref[...]
        tq = tq_ref[...]
        lane = lax.broadcasted_iota(jnp.int32, (ht, LANES), 1)
        lane_f = lane.astype(F32)
        gate = jnp.dot(qbdf_sc[...], ksum_sc[...], precision=HIGHEST, preferred_element_type=F32)
        gate = jnp.where(lane < n_blocks, gate, -jnp.inf)
        keep = _top_k_mask(gate, 0, n_blocks, min(MOBA_TOP_K, n_blocks)) & (lane < n_blocks)
        sel_bias = jnp.where(keep, 0.0, NEG)
        for n in range(n_blocks):
            selb_sc[n] = jnp.broadcast_to(sel_bias[:, n:n + 1], (ht, LANES))

        k_own = jnp.concatenate([kn_ref[...], jnp.zeros((LANES - t_new, W_ATT), F32)], axis=0)
        so = lax.dot_general(qbd_sc[...], k_own.astype(BF16), NT_DIMS, preferred_element_type=F32)
        lo = jnp.where((lane_f <= tq) & (lane < t_new), so - slope * (tq - lane_f), NEG)
        qpos = tq + float(past)

        def logits(n, pp):
            pidx = n * ppb + pp
            dist = qpos - (lane_f + jnp.asarray(pidx * page, F32))
            return pidx, st_sc[pidx] - slope * dist + selb_sc[n]

        def max_body(n, m_vec):
            for pp in range(ppb):
                m_vec = jnp.maximum(m_vec, logits(n, pp)[1])
            return m_vec

        m = jnp.max(lax.fori_loop(0, n_blocks, max_body, lo, unroll=4), axis=1, keepdims=True)
        eo = jnp.exp(lo - m)

        def exp_body(n, l_vec):
            for pp in range(ppb):
                pidx, lg = logits(n, pp)
                e = jnp.exp(lg - m)
                st_sc[pidx] = e
                l_vec = l_vec + e
            return l_vec

        l_vec = lax.fori_loop(0, n_blocks, exp_body, eo, unroll=4)
        inv = 1.0 / jnp.sum(l_vec, axis=1, keepdims=True)
        inv_sc[...] = jnp.broadcast_to(inv, inv_sc.shape)
        own_sc[...] = eo * inv
        acc_sc[...] = jnp.zeros_like(acc_sc)

    @pl.when(ph == 1)
    def _():
        inv = inv_sc[...]
        tot = None
        for ii in range(PAGES_PER_STEP):
            p = (st_sc[pg * PAGES_PER_STEP + ii] * inv).astype(BF16)
            d = lax.dot_general(p, v_refs[ii][...].astype(BF16), NT_DIMS, preferred_element_type=F32)
            tot = d if tot is None else tot + d
        acc_sc[...] += tot

    @pl.when((ph == 1) & (pg == n_groups - 1))
    def _():
        v_own = jnp.concatenate([vn_ref[...], jnp.zeros((LANES - t_new, W_ATT), F32)], axis=0)
        acc = acc_sc[...] + jnp.dot(own_sc[...].astype(BF16), v_own.astype(BF16), preferred_element_type=F32)
        col = lax.broadcasted_iota(jnp.int32, (t_new, W_ATT), 1)
        out = jnp.zeros((t_new, W_ATT), F32)
        for h in range(N_HEADS):
            out += jnp.where(col // HEAD_DIM == h, acc[h * t_new:(h + 1) * t_new, :], 0.0)
        o_ref[...] = out


def _moba_sample(q, k, v, cache_kt, cache_vt, layer, page_table, slopes_np, bsz, t_new):
    page = cache_kt.shape[-1]
    n_pages = page_table.shape[1]
    assert MOBA_BLOCK % page == 0 and (n_pages * page) % MOBA_BLOCK == 0 and n_pages % PAGES_PER_STEP == 0
    assert t_new % SUBLANES == 0 and t_new <= LANES and page == LANES
    ht = N_HEADS * t_new
    n_groups = n_pages // PAGES_PER_STEP
    n_blocks = n_pages * page // MOBA_BLOCK
    assert n_blocks <= LANES
    row_h = np.arange(ht) // t_new
    slope_r = jnp.asarray(np.broadcast_to(slopes_np[row_h][:, None], (ht, LANES)), F32)
    tq_r = jnp.asarray(np.broadcast_to((np.arange(ht) % t_new)[:, None], (ht, LANES)), F32)
    wide = lambda a: a.reshape(t_new, bsz * W_ATT)
    tok_spec = pl.BlockSpec((t_new, W_ATT), lambda b, ph, pg, pt: (0, b))
    const_spec = pl.BlockSpec((ht, LANES), lambda b, ph, pg, pt: (0, 0))

    def k_map(ii):
        def f(b, ph, pg, pt):
            p = jnp.where(ph == 0, pg, n_groups - 1) * PAGES_PER_STEP + ii
            return (layer, pt[b * n_pages + p], 0, 0)
        return f

    def v_map(ii):
        def f(b, ph, pg, pt):
            p = jnp.where(ph == 1, pg, 0) * PAGES_PER_STEP + ii
            return (layer, pt[b * n_pages + p], 0, 0)
        return f

    page_block = (None, None, W_ATT, page)
    grid_spec = pltpu.PrefetchScalarGridSpec(
        num_scalar_prefetch=1,
        grid=(bsz, 2, n_groups),
        in_specs=[const_spec, const_spec, tok_spec, tok_spec, tok_spec]
                 + [pl.BlockSpec(page_block, k_map(ii)) for ii in range(PAGES_PER_STEP)]
                 + [pl.BlockSpec(page_block, v_map(ii)) for ii in range(PAGES_PER_STEP)],
        out_specs=tok_spec,
        scratch_shapes=[pltpu.VMEM((n_pages, ht, page), F32),
                        pltpu.VMEM((n_blocks, ht, LANES), F32),
                        pltpu.VMEM((W_ATT, LANES), F32),
                        pltpu.VMEM((ht, W_ATT), BF16),
                        pltpu.VMEM((ht, W_ATT), F32),
                        pltpu.VMEM((ht, LANES), F32),
                        pltpu.VMEM((ht, LANES), F32),
                        pltpu.VMEM((ht, W_ATT), F32)],
    )
    out = pl.pallas_call(
        functools.partial(_moba_sample_kernel, n_pages=n_pages, page=page, t_new=t_new),
        grid_spec=grid_spec,
        out_shape=jax.ShapeDtypeStruct((t_new, bsz * W_ATT), F32),
        compiler_params=_cparams(("parallel", "arbitrary", "arbitrary")),
    )(page_table.reshape(-1), slope_r, tq_r, wide(q), wide(k), wide(v), *([cache_kt] * PAGES_PER_STEP),
      *([cache_vt] * PAGES_PER_STEP))
    return out.reshape(t_new * bsz, W_ATT)


SCAN_LANES = 256


def _gelu_tanh(x):
    return 0.5 * x * (1.0 + jnp.tanh(np.float32(np.sqrt(2.0 / np.pi)) * (x + 0.044715 * (x * x * x))))


def _ssm_kernel(u_ref, h0r_ref, h0i_ref, a_ref, bbd_ref, cbd_ref, d_ref, wglu_ref,
                y_ref, hr_ref, hi_ref, xs_sc, hr_sc, hi_sc, *wide_sc, bsz, t_chunk):
    step = pl.program_id(0)

    @pl.when(step == 0)
    def _():
        hr_sc[...] = h0r_ref[...]
        hi_sc[...] = h0i_ref[...]

    n_half = W_SSM // LANES
    if wide_sc:
        u_scs, y_scs = wide_sc[:n_half], wide_sc[n_half:]
        for b in range(bsz):
            for hf in range(n_half):
                col = b * W_SSM + hf * LANES
                u_scs[hf][pl.ds(b, t_chunk, stride=bsz), :] = u_ref[:, col:col + LANES]
        u = jnp.concatenate([sc[...] for sc in u_scs], axis=1)
    else:
        u = u_ref[...]
    xs_sc[...] = jnp.dot(u.astype(BF16), bbd_ref[...], preferred_element_type=F32)
    for lc in range(N_STATE // SCAN_LANES):
        re = slice(lc * SCAN_LANES, (lc + 1) * SCAN_LANES)
        im = slice(N_STATE + lc * SCAN_LANES, N_STATE + (lc + 1) * SCAN_LANES)
        ar = jnp.broadcast_to(a_ref[0:1, re], (bsz, SCAN_LANES))
        ai = jnp.broadcast_to(a_ref[1:2, re], (bsz, SCAN_LANES))

        def scan_step(t, carry, re=re, im=im, ar=ar, ai=ai):
            hr, hi = carry
            rows = pl.ds(pl.multiple_of(t * bsz, bsz), bsz)
            nr = ar * hr - ai * hi + xs_sc[rows, re]
            ni = ar * hi + ai * hr + xs_sc[rows, im]
            xs_sc[rows, re] = nr
            xs_sc[rows, im] = ni
            return nr, ni

        hr, hi = lax.fori_loop(0, t_chunk, scan_step, (hr_sc[:, re], hi_sc[:, re]))
        hr_sc[:, re] = hr
        hi_sc[:, re] = hi

    y = jnp.dot(xs_sc[...].astype(BF16), cbd_ref[...], preferred_element_type=F32) + d_ref[...] * u
    y = _gelu_tanh(y)
    y = y * jax.nn.sigmoid(jnp.dot(y.astype(BF16), wglu_ref[...], preferred_element_type=F32))
    if wide_sc:
        for hf in range(n_half):
            y_scs[hf][...] = y[:, hf * LANES:(hf + 1) * LANES]
        for b in range(bsz):
            for hf in range(n_half):
                col = b * W_SSM + hf * LANES
                y_ref[:, col:col + LANES] = y_scs[hf][pl.ds(b, t_chunk, stride=bsz), :]
    else:
        y_ref[...] = y

    @pl.when(step == pl.num_programs(0) - 1)
    def _():
        hr_ref[...] = hr_sc[...]
        hi_ref[...] = hi_sc[...]


def _ssm(u, h0r, h0i, a, bbd, cbd, d, wglu, bsz, seq, t_chunk):
    assert bsz % SUBLANES == 0 and seq % t_chunk == 0 and u.shape[0] in (seq, seq * bsz)
    rows = t_chunk * bsz
    wide = u.shape[0] == seq
    blk = pl.BlockSpec((t_chunk, bsz * W_SSM) if wide else (rows, W_SSM), lambda s: (s, 0))
    st_spec = _full((bsz, N_STATE))
    scratch = [pltpu.VMEM((rows, 2 * N_STATE), F32), pltpu.VMEM((bsz, N_STATE), F32), pltpu.VMEM((bsz, N_STATE), F32)]
    if wide:
        scratch += [pltpu.VMEM((rows, LANES), F32)] * (2 * (W_SSM // LANES))
    return pl.pallas_call(
        functools.partial(_ssm_kernel, bsz=bsz, t_chunk=t_chunk),
        grid=(seq // t_chunk,),
        in_specs=[blk, st_spec, st_spec, _full(a.shape), _full(bbd.shape), _full(cbd.shape), _full(d.shape),
                  _full(wglu.shape)],
        out_specs=[blk, st_spec, st_spec],
        out_shape=[jax.ShapeDtypeStruct(u.shape, F32), jax.ShapeDtypeStruct((bsz, N_STATE), F32),
                   jax.ShapeDtypeStruct((bsz, N_STATE), F32)],
        scratch_shapes=scratch,
        compiler_params=_cparams(("arbitrary",)),
    )(u, h0r, h0i, a, bbd, cbd, d, wglu)


def _mix_out_kernel(*refs, tm, halo, row_stride, tiles_per_seq):
    if halo:
        (att_ref, cb_ref, cc_ref, cx_ref, cch_ref, cxh_ref, zst_ref, y_ref, x_ref, gb_ref, wout_ref, cw_ref,
         cbias_ref, xo_ref, zlast_ref, zs_sc) = refs
    else:
        (att_ref, cb_ref, cc_ref, cx_ref, zst_ref, y_ref, x_ref, gb_ref, wout_ref, cw_ref,
         cbias_ref, xo_ref, zlast_ref, zs_sc) = refs
    pre = zs_sc.shape[0] - tm
    z = cc_ref[...] * cx_ref[...]
    if halo:
        first = pl.program_id(0) % tiles_per_seq == 0
        zs_sc[0:pre, :] = jnp.where(first, zst_ref[...], cch_ref[...] * cxh_ref[...])
    else:
        zs_sc[0:pre, :] = zst_ref[...]
    zs_sc[pre:, :] = z
    yc = cbias_ref[...]
    for tap in range(CONV_WIDTH - 1):
        off = pre - (CONV_WIDTH - 1 - tap) * row_stride
        yc = yc + cw_ref[tap:tap + 1, :] * zs_sc[off:off + tm, :]
    yc = yc + cw_ref[CONV_WIDTH - 1:CONV_WIDTH, :] * z
    conv_out = cb_ref[...] * yc
    zlast_ref[...] = zs_sc[tm:tm + pre, :]

    acc = x_ref[...]
    col = 0
    for branch in (att_ref[...], conv_out, y_ref[...]):
        width = branch.shape[-1]
        nb = (_rms(branch) * gb_ref[:, col:col + width]).astype(BF16)
        acc = acc + jnp.dot(nb, wout_ref[col:col + width, :], preferred_element_type=F32)
        col += width
    xo_ref[...] = acc


def _mix_out(att, cb, cc, cx, zstate, y_arr, y_spec, x, gb, wout, cw, cbias, tm, halo, row_stride, tiles_per_seq,
             zstate_spec):
    n = x.shape[0]
    pre = zstate_spec.block_shape[-2]
    rows = lambda w: pl.BlockSpec((tm, w), lambda i: (i, 0))
    in_specs = [rows(W_ATT), rows(W_CONV), rows(W_CONV), rows(W_CONV)]
    args = [att, cb, cc, cx]
    if halo:
        per = tm // SUBLANES
        halo_spec = pl.BlockSpec((SUBLANES, W_CONV), lambda i: (jnp.maximum(i * per - 1, 0), 0))
        in_specs += [halo_spec, halo_spec]
        args += [cc, cx]
    in_specs += [zstate_spec, y_spec, rows(D_MODEL), _full(gb.shape), _full(wout.shape), _full(cw.shape),
                 _full(cbias.shape)]
    args += [zstate, y_arr, x, gb, wout, cw, cbias]
    n_tiles = n // tm
    return pl.pallas_call(
        functools.partial(_mix_out_kernel, tm=tm, halo=halo, row_stride=row_stride, tiles_per_seq=tiles_per_seq),
        grid=(n_tiles,),
        in_specs=in_specs,
        out_specs=[rows(D_MODEL), pl.BlockSpec((pre, W_CONV), lambda i: (i, 0))],
        out_shape=[jax.ShapeDtypeStruct((n, D_MODEL), F32), jax.ShapeDtypeStruct((n_tiles * pre, W_CONV), F32)],
        scratch_shapes=[pltpu.VMEM((tm + pre, W_CONV), F32)],
        compiler_params=_cparams(("parallel",)),
    )(*args)


def _ffn_kernel(*refs, moe, final):
    refs = list(refs)
    x_ref, g_ref = refs[:2]
    pos = 2
    if moe:
        wr_ref, br_ref = refs[pos:pos + 2]
        pos += 2
    wg_ref, wu_ref, wd_ref = refs[pos:pos + 3]
    pos += 3
    if final:
        gf_ref = refs[pos]
        pos += 1
    o_ref, h_sc, acc_sc = refs[pos:pos + 3]
    pos += 3
    if moe:
        comb_sc = refs[pos]
    e = pl.program_id(1)
    c = pl.program_id(2)
    last = (e == pl.num_programs(1) - 1) & (c == pl.num_programs(2) - 1)

    @pl.when((e == 0) & (c == 0))
    def _():
        h = _rms(x_ref[...]) * g_ref[...]
        h_sc[...] = h.astype(BF16)
        acc_sc[...] = jnp.zeros_like(acc_sc)
        if moe:
            logits = _dot_split(h, wr_ref[...]) + br_ref[...]
            lane = lax.broadcasted_iota(jnp.int32, logits.shape, 1)
            logits = jnp.where(lane < N_EXPERTS, logits, -jnp.inf)
            keep = _top_k_mask(logits, 0, N_EXPERTS, EXPERT_TOP_K) & (lane < N_EXPERTS)
            top = jnp.max(logits, axis=1, keepdims=True)
            w = jnp.where(keep, jnp.exp(logits - top), 0.0)
            comb_sc[...] = w / jnp.sum(w, axis=1, keepdims=True)

    hb = h_sc[...]
    a = jnp.dot(hb, wg_ref[...], preferred_element_type=F32)
    b = jnp.dot(hb, wu_ref[...], preferred_element_type=F32)
    t = (a * jax.nn.sigmoid(a) * b).astype(BF16)
    d = jnp.dot(t, wd_ref[...], preferred_element_type=F32)
    if moe:
        d = d * _lane_column(comb_sc[...], e)
    acc_sc[...] += d

    @pl.when(last)
    def _():
        out = x_ref[...] + acc_sc[...]
        if final:
            out = _rms(out) * gf_ref[...]
        o_ref[...] = out


def _ffn(x, g, wg, wu, wd, tm, f_chunk, router=None, g_final=None):
    n = x.shape[0]
    n_exp, f_dim, _ = wd.shape
    up_block = 0
    if wu is None:
        wu, up_block = wg, f_dim // f_chunk
    moe = router is not None
    final = g_final is not None
    rows = pl.BlockSpec((tm, D_MODEL), lambda i, e, c: (i, 0))
    const = lambda shape: pl.BlockSpec(shape, lambda i, e, c: (0,) * len(shape))
    in_specs = [rows, const((1, D_MODEL))]
    args = [x, g]
    if moe:
        in_specs += [const(router[0].shape), const(router[1].shape)]
        args += list(router)
    w_mode = dict(pipeline_mode=pl.Buffered(1)) if n_exp == 1 and f_chunk == f_dim else {}
    in_specs += [pl.BlockSpec((None, D_MODEL, f_chunk), lambda i, e, c: (e, 0, c), **w_mode),
                 pl.BlockSpec((None, D_MODEL, f_chunk), lambda i, e, c: (e, 0, up_block + c), **w_mode),
                 pl.BlockSpec((None, f_chunk, D_MODEL), lambda i, e, c: (e, c, 0), **w_mode)]
    args += [wg, wu, wd]
    if final:
        in_specs.append(const((1, D_MODEL)))
        args.append(g_final)
    scratch = [pltpu.VMEM((tm, D_MODEL), BF16), pltpu.VMEM((tm, D_MODEL), F32)]
    if moe:
        scratch.append(pltpu.VMEM((tm, LANES), F32))
    return pl.pallas_call(
        functools.partial(_ffn_kernel, moe=moe, final=final),
        grid=(n // tm, n_exp, f_dim // f_chunk),
        in_specs=in_specs,
        out_specs=rows,
        out_shape=jax.ShapeDtypeStruct((n, D_MODEL), F32),
        scratch_shapes=scratch,
        compiler_params=_cparams(("parallel", "arbitrary", "arbitrary")),
    )(*args)


MOE_TILE = 1024
MOE_CHUNKS = (192, 256, 320, 384)


def _router_kernel(x_ref, g_ref, wr_ref, br_ref, h_ref, comb_ref, cnt_ref):
    h = _rms(x_ref[...]) * g_ref[...]
    h_ref[...] = h.astype(BF16)
    logits = _dot_split(h, wr_ref[...]) + br_ref[...]
    lane = lax.broadcasted_iota(jnp.int32, logits.shape, 1)
    logits = jnp.where(lane < N_EXPERTS, logits, -jnp.inf)
    keep = _top_k_mask(logits, 0, N_EXPERTS, EXPERT_TOP_K) & (lane < N_EXPERTS)
    w = jnp.where(keep, jnp.exp(logits - jnp.max(logits, axis=1, keepdims=True)), 0.0)
    comb = w / jnp.sum(w, axis=1, keepdims=True)
    comb_ref[...] = comb
    count = jnp.sum(jnp.where(comb > 0.0, 1.0, 0.0), axis=0, keepdims=True)
    cnt_ref[...] = jnp.broadcast_to(count, cnt_ref.shape)


def _router(x, g, wr, br, tm):
    n = x.shape[0]
    n_tiles = n // tm
    h, comb, cnt = pl.pallas_call(
        _router_kernel,
        grid=(n_tiles,),
        in_specs=[pl.BlockSpec((tm, D_MODEL), lambda i: (i, 0)), _full((1, D_MODEL)), _full(wr.shape), _full(br.shape)],
        out_specs=[pl.BlockSpec((tm, D_MODEL), lambda i: (i, 0)), pl.BlockSpec((tm, LANES), lambda i: (i, 0)),
                   pl.BlockSpec((SUBLANES, LANES), lambda i: (i, 0))],
        out_shape=[jax.ShapeDtypeStruct((n, D_MODEL), BF16), jax.ShapeDtypeStruct((n, LANES), F32),
                   jax.ShapeDtypeStruct((n_tiles * SUBLANES, LANES), F32)],
        compiler_params=_cparams(("parallel",)),
    )(x, g, wr, br)
    counts = cnt.reshape(n_tiles, SUBLANES, LANES)[:, 0, :N_EXPERTS].astype(jnp.int32).reshape(-1)
    return h, comb, counts


def _moe_kernel(cnt_ref, x_ref, h_ref, comb_ref, wgu_ref, wd_ref, gf_ref, o_ref,
                tri_sc, rank_sc, maskt_sc, rankt_sc, *, tm, chunks, final):
    i = pl.program_id(0)
    e = pl.program_id(1)
    sub = 256

    @pl.when((i == 0) & (e == 0))
    def _():
        for rc in range(tm // sub):
            r = lax.broadcasted_iota(jnp.int32, (sub, tm), 0) + rc * sub
            c = lax.broadcasted_iota(jnp.int32, (sub, tm), 1)
            tri_sc[rc * sub:(rc + 1) * sub, :] = jnp.where(c < r, 1.0, 0.0).astype(BF16)

    @pl.when(e == 0)
    def _():
        routed = jnp.where(comb_ref[...] > 0.0, 1.0, 0.0).astype(BF16)
        rank_sc[...] = jnp.dot(tri_sc[...], routed, preferred_element_type=F32)
        er = lax.broadcasted_iota(jnp.int32, (LANES, LANES), 0)
        ec = lax.broadcasted_iota(jnp.int32, (LANES, LANES), 1)
        eye = jnp.where(er == ec, 1.0, 0.0).astype(BF16)
        routed_t = lax.dot_general(eye, routed, NT_DIMS, preferred_element_type=F32)
        rank_t = lax.dot_general(routed_t.astype(BF16), tri_sc[...], NT_DIMS, preferred_element_type=F32)
        maskt_sc[...] = routed_t[0:SUBLANES]
        rankt_sc[...] = rank_t[0:SUBLANES]
        o_ref[...] = x_ref[...]

    count = cnt_ref[i * N_EXPERTS + e]
    gate_col = _lane_column(comb_ref[...], e)
    rank_col = _lane_column(rank_sc[...], e)
    routed_row = maskt_sc[pl.ds(e, 1), :]
    rank_row = rankt_sc[pl.ds(e, 1), :]
    def run_chunk(size, base):
        size_pad = -(-size // LANES) * LANES
        slot = lax.broadcasted_iota(jnp.int32, (size, tm), 0).astype(F32) + float(base)
        pick = jnp.where((rank_row == slot) & (routed_row > 0.5), 1.0, 0.0).astype(BF16)
        xg = jnp.dot(pick, h_ref[...], preferred_element_type=F32).astype(BF16)
        ab = jnp.dot(xg, wgu_ref[...], preferred_element_type=F32)
        f_dim = wd_ref.shape[0]
        a, b = ab[:, 0:f_dim], ab[:, f_dim:2 * f_dim]
        t = (a * jax.nn.sigmoid(a) * b).astype(BF16)
        y = jnp.dot(t, wd_ref[...], preferred_element_type=F32).astype(BF16)
        if size_pad > size:
            y = jnp.concatenate([y, jnp.zeros((size_pad - size, D_MODEL), BF16)], axis=0)
        slot_l = lax.broadcasted_iota(jnp.int32, (tm, size_pad), 1).astype(F32) + float(base)
        place = jnp.where((rank_col == slot_l) & (gate_col > 0.0), 1.0, 0.0).astype(BF16)
        o_ref[...] += gate_col * jnp.dot(place, y, preferred_element_type=F32)

    largest = chunks[-1]
    for idx, size in enumerate(chunks):
        fits_smaller = chunks[idx - 1] if idx > 0 else 0
        cond = count > fits_smaller
        if size != largest:
            cond = cond & (count <= size)
        pl.when(cond)(functools.partial(run_chunk, size, 0))
    for ch in range(1, -(-tm // largest)):
        pl.when(count > ch * largest)(functools.partial(run_chunk, largest, ch * largest))

    if final:
        @pl.when(e == pl.num_programs(1) - 1)
        def _():
            o_ref[...] = _rms(o_ref[...]) * gf_ref[...]


def _moe_sparse(x, g, wgu, wd, router, g_final=None):
    n = x.shape[0]
    tm, chunks = MOE_TILE, tuple(MOE_CHUNKS)
    assert n % tm == 0 and tm % 256 == 0 and list(chunks) == sorted(set(chunks))
    assert all(c % (2 * SUBLANES) == 0 for c in chunks)
    n_exp, f_dim, _ = wd.shape
    assert n_exp <= SUBLANES and wgu.shape[-1] == 2 * f_dim
    h, comb, counts = _router(x, g, router[0], router[1], tm)
    final = g_final is not None
    gf = g_final if final else jnp.ones((1, D_MODEL), F32)
    rows = lambda w: pl.BlockSpec((tm, w), lambda i, e, cnt: (i, 0))
    grid_spec = pltpu.PrefetchScalarGridSpec(
        num_scalar_prefetch=1,
        grid=(n // tm, n_exp),
        in_specs=[rows(D_MODEL), rows(D_MODEL), rows(LANES),
                  pl.BlockSpec((None, D_MODEL, 2 * f_dim), lambda i, e, cnt: (e, 0, 0)),
                  pl.BlockSpec((None, f_dim, D_MODEL), lambda i, e, cnt: (e, 0, 0)),
                  pl.BlockSpec((1, D_MODEL), lambda i, e, cnt: (0, 0))],
        out_specs=rows(D_MODEL),
        scratch_shapes=[pltpu.VMEM((tm, tm), BF16),
                        pltpu.VMEM((tm, LANES), F32),
                        pltpu.VMEM((SUBLANES, tm), F32),
                        pltpu.VMEM((SUBLANES, tm), F32)],
    )
    return pl.pallas_call(
        functools.partial(_moe_kernel, tm=tm, chunks=chunks, final=final),
        grid_spec=grid_spec,
        out_shape=jax.ShapeDtypeStruct((n, D_MODEL), F32),
        compiler_params=_cparams(("arbitrary", "arbitrary")),
    )(counts, x, h, comb, wgu, wd, gf)


def _alibi_slopes_np():
    return (2.0 ** (-8.0 * np.arange(1, N_HEADS + 1) / N_HEADS)).astype(np.float32)


def _ssm_params(a_re, a_im, log_dt, b_re, b_im, c_re, c_im):
    ar, ai = a_re.astype(F32), a_im.astype(F32)
    dt = jnp.exp(log_dt.astype(F32))[:, None]
    mag = jnp.exp(dt * ar)
    abr, abi = mag * jnp.cos(dt * ai), mag * jnp.sin(dt * ai)
    den = ar * ar + ai * ai
    zr = ((abr - 1.0) * ar + abi * ai) / den
    zi = (abi * ar - (abr - 1.0) * ai) / den
    br, bim = b_re.astype(F32), b_im.astype(F32)
    bbr = zr[..., None] * br - zi[..., None] * bim
    bbi = zr[..., None] * bim + zi[..., None] * br
    eye = jnp.eye(N_GROUPS, dtype=F32)
    to_in = lambda m: jnp.einsum('gnc,gh->gchn', m, eye).reshape(W_SSM, N_STATE)
    to_out = lambda m: jnp.einsum('gcn,gh->gnhc', m, eye).reshape(N_STATE, W_SSM)
    bbd = jnp.concatenate([to_in(bbr), to_in(bbi)], axis=1).astype(BF16)
    cbd = jnp.concatenate([to_out(c_re.astype(F32)), -to_out(c_im.astype(F32))], axis=0).astype(BF16)
    a = jnp.stack([abr.reshape(N_STATE), abi.reshape(N_STATE)])
    return a, bbd, cbd


def _trunk(x, w, depth, bsz, seq, time_major, attend, conv0, h0r, h0i):
    n = bsz * seq
    ks, vs, zs, hrs, his = [], [], [], [], []
    if time_major:
        tm, tiles_per_seq, row_stride, halo = n, 1, bsz, False
        pre = (CONV_WIDTH - 1) * bsz
        u_shape = (n, W_SSM)
        u_spec = pl.BlockSpec((tm, W_SSM), lambda i: (0, 0))
        kv_shape = (n, W_ATT)
        kv_spec = pl.BlockSpec((tm, W_ATT), lambda i: (0, 0))
        zstate_spec = pl.BlockSpec((pre, W_CONV), lambda i: (0, 0))
        t_chunk = seq
    else:
        tm, row_stride, halo = ROW_TILE, 1, True
        tiles_per_seq = seq // tm
        pre = SUBLANES
        u_shape = (seq, bsz * W_SSM)
        u_spec = pl.BlockSpec((tm, W_SSM), lambda i: (i % tiles_per_seq, i // tiles_per_seq))
        kv_shape = (depth, bsz, W_ATT, seq)
        zstate_spec = pl.BlockSpec((None, pre, W_CONV), lambda i: (i // tiles_per_seq, 0, 0))
        t_chunk = 64
    k, v = (None, None) if time_major else (jnp.zeros(kv_shape, F32), jnp.zeros(kv_shape, F32))
    for l in range(depth):
        if not time_major:
            kv_spec = pl.BlockSpec((None, None, W_ATT, tm),
                                   lambda i, l=l: (l, i // tiles_per_seq, 0, i % tiles_per_seq))
        q, k, v, cb, cc, cx, u = _mix_in(x, w['g_mix'][l], w['w_in'][l], w['w_kvt'][l], tm, u_shape, u_spec,
                                         kv_shape, kv_spec, kv_prev=None if time_major else (k, v))
        att = attend(l, q, k, v)
        y, hr, hi = _ssm(u, h0r[l], h0i[l], *w['ssm'][l], bsz, seq, t_chunk)
        x, zlast = _mix_out(att, cb, cc, cx, conv0[l], y, u_spec, x, w['g_branch'][l], w['w_out'][l],
                            w['conv_w'][l], w['conv_b'][l], tm, halo, row_stride, tiles_per_seq, zstate_spec)
        g_final = w['g_final'] if l == depth - 1 else None
        if l % 2 == 0:
            x = _ffn(x, w['g_ffn'][l], *w['dense'][l // 2], tm, w['dense'][l // 2][0].shape[-1], g_final=g_final)
        else:
            wgu, wd, router = w['moe'][l // 2]
            if x.shape[0] % MOE_TILE == 0:
                x = _moe_sparse(x, w['g_ffn'][l], wgu, wd, router, g_final=g_final)
            else:
                x = _ffn(x, w['g_ffn'][l], wgu, None, wd, tm, wd.shape[1], router=router, g_final=g_final)
        ks.append(k)
        vs.append(v)
        zs.append(zlast)
        hrs.append(hr)
        his.append(hi)
    return x, ks, vs, zs, hrs, his


def kernel(x_prompt, x_sample, cache_k, cache_v, state_conv, state_ssm_re, state_ssm_im, page_table, g_mix_norm, w_in, g_branch, w_out, conv_w, conv_b, ssm_a_re, ssm_a_im, ssm_log_dt, ssm_b_re, ssm_b_im, ssm_c_re, ssm_c_im, ssm_d, ssm_w_glu, g_ffn_norm, w_ffn_gate, w_ffn_up, w_ffn_down, w_router, b_router, w_exp_gate, w_exp_up, w_exp_down, g_final):
    depth = w_in.shape[0]
    bp, sp, _ = x_prompt.shape
    bs, ss, _ = x_sample.shape
    n_pool, page = cache_k.shape[1], cache_k.shape[2]
    slopes_np = _alibi_slopes_np()
    slopes = jnp.asarray(slopes_np)

    row = lambda a: a.reshape(a.shape[0], 1, a.shape[-1])
    w_in_bf16 = w_in.astype(BF16)
    w = {
        'g_mix': row(g_mix_norm), 'w_in': w_in_bf16, 'g_branch': row(g_branch), 'w_out': w_out.astype(BF16),
        'w_kvt': jnp.swapaxes(w_in_bf16[:, :, W_ATT:3 * W_ATT], 1, 2),
        'conv_w': conv_w, 'conv_b': row(conv_b), 'g_ffn': row(g_ffn_norm), 'g_final': g_final.reshape(1, D_MODEL),
        'ssm': [(*_ssm_params(ssm_a_re[l], ssm_a_im[l], ssm_log_dt[l], ssm_b_re[l], ssm_b_im[l], ssm_c_re[l],
                              ssm_c_im[l]), ssm_d[l].reshape(1, W_SSM), ssm_w_glu[l].astype(BF16))
                for l in range(depth)],
        'dense': [(w_ffn_gate[j:j + 1].astype(BF16), w_ffn_up[j:j + 1].astype(BF16), w_ffn_down[j:j + 1].astype(BF16))
                  for j in range(w_ffn_gate.shape[0])],
        'moe': [(jnp.concatenate([w_exp_gate[j], w_exp_up[j]], axis=-1).astype(BF16), w_exp_down[j].astype(BF16),
                 (jnp.pad(w_router[j], ((0, 0), (0, LANES - N_EXPERTS))),
                  jnp.pad(b_router[j].reshape(1, N_EXPERTS), ((0, 0), (0, LANES - N_EXPERTS)))))
                for j in range(w_exp_gate.shape[0])],
    }

    conv0_p = jnp.zeros((depth, bp, SUBLANES, W_CONV), F32)
    h0_p = jnp.zeros((depth, bp, N_STATE), F32)
    attend_p = lambda l, q, kt, vt: _moba_prompt(q, kt, vt, l, slopes, bp, sp)
    yp, kts, vts, zs, hrs, his = _trunk(x_prompt.reshape(bp * sp, D_MODEL), w, depth, bp, sp, False, attend_p,
                                        conv0_p, h0_p, h0_p)
    y_prompt = yp.reshape(bp, sp, D_MODEL)
    from_t = lambda a: jnp.transpose(a[-1].reshape(depth, bp, N_HEADS, HEAD_DIM, sp), (0, 1, 4, 2, 3))
    k_p, v_p = from_t(kts), from_t(vts)
    tiles = sp // ROW_TILE
    conv_p = jnp.stack(zs).reshape(depth, bp, tiles, SUBLANES, W_CONV)[:, :, -1, SUBLANES - (CONV_WIDTH - 1):, :]
    sre_p = jnp.stack(hrs).reshape(depth, bp, N_GROUPS, SSM_STATE)
    sim_p = jnp.stack(his).reshape(depth, bp, N_GROUPS, SSM_STATE)

    to_t = lambda c: jnp.transpose(c, (0, 1, 3, 4, 2)).reshape(depth, n_pool, W_ATT, page)
    cache_kt, cache_vt = to_t(cache_k), to_t(cache_v)
    xs = jnp.swapaxes(x_sample, 0, 1).reshape(ss * bs, D_MODEL)
    conv0_s = jnp.swapaxes(state_conv, 1, 2).reshape(depth, (CONV_WIDTH - 1) * bs, W_CONV)
    h0r_s = state_ssm_re.reshape(depth, bs, N_STATE)
    h0i_s = state_ssm_im.reshape(depth, bs, N_STATE)
    attend_s = lambda l, q, k, v: _moba_sample(q, k, v, cache_kt, cache_vt, l, page_table, slopes_np, bs, ss)
    ys, ks, vs, zs, hrs, his = _trunk(xs, w, depth, bs, ss, True, attend_s, conv0_s, h0r_s, h0i_s)
    from_tb = lambda a, width: jnp.swapaxes(a.reshape(-1, ss, bs, width), 1, 2)
    y_sample = from_tb(ys, D_MODEL)[0]
    k_s = from_tb(jnp.stack(ks), W_ATT).reshape(depth, bs, ss, N_HEADS, HEAD_DIM)
    v_s = from_tb(jnp.stack(vs), W_ATT).reshape(depth, bs, ss, N_HEADS, HEAD_DIM)
    conv_s = jnp.swapaxes(jnp.stack(zs).reshape(depth, CONV_WIDTH - 1, bs, W_CONV), 1, 2)
    sre_s = jnp.stack(hrs).reshape(depth, bs, N_GROUPS, SSM_STATE)
    sim_s = jnp.stack(his).reshape(depth, bs, N_GROUPS, SSM_STATE)
    return (y_prompt, y_sample, k_p, v_p, conv_p, sre_p, sim_p, k_s, v_s, conv_s, sre_s, sim_s)
```
